```python
import jax, jax.numpy as jnp
from jax import lax
import numpy as np

D_MODEL = 1024
BATCH = 16
SEQ = 2048
DEPTH = 1
DEC_BATCH = 32
DEC_SEQ = 16
PAST_LEN = 1024

CHUNK = 64
QBLK = 128
N_HEADS_A = 8
DH_A = 64
N_IDX = 4
D_IDX = 64
TOPK_MAX = 256
NH_M = 4
DK_M = 64
DV_M = 128
D_FF = 2816
EPS = 1e-6
IDX_SCALE = (N_IDX * D_IDX) ** -0.5
SPLITS = (N_HEADS_A * DH_A, DH_A, DH_A, N_IDX * D_IDX, D_IDX, N_IDX,
          NH_M * DK_M, NH_M * DK_M, NH_M * DV_M, 2 * NH_M, NH_M * DV_M, 2 * D_MODEL)
D_IN = N_HEADS_A * DH_A + 2 * DH_A + N_IDX * D_IDX + D_IDX + N_IDX + 2 * NH_M * DK_M + 2 * NH_M * DV_M + 2 * NH_M + 2 * D_MODEL

kernel_name = 'streaming_dsa_mlstm_macaron_step'


def rmsnorm(x, g):
    x32 = x.astype(jnp.float32)
    y = x32 * lax.rsqrt(jnp.mean(x32 * x32, axis=-1, keepdims=True) + EPS)
    return y.astype(x.dtype) * g


def half_ffn(x, g, w_up, w_down):
    a, b = jnp.split(rmsnorm(x, g) @ w_up, 2, axis=-1)
    return x + 0.5 * ((jax.nn.silu(a) * b) @ w_down)


def alibi_slopes():
    return jnp.exp2(-8.0 * jnp.arange(1, N_HEADS_A + 1, dtype=jnp.float32) / N_HEADS_A)


def project(h, w_in, b_gates):
    B, T, _ = h.shape
    offs = np.cumsum(SPLITS)[:-1].tolist()
    aq, ak, av, iq, ik, iw, mq, mk, mv, mif, mo, gates = jnp.split(h @ w_in, offs, axis=-1)
    mif = mif.astype(jnp.float32) + b_gates.astype(jnp.float32)
    ig = mif[..., :NH_M]
    lf = jax.nn.log_sigmoid(mif[..., NH_M:])
    g_a, g_b = jnp.split(gates, 2, axis=-1)
    return (aq.reshape(B, T, N_HEADS_A, DH_A), ak, av, iq.reshape(B, T, N_IDX, D_IDX), ik, iw,
            mq.reshape(B, T, NH_M, DK_M), mk.reshape(B, T, NH_M, DK_M), mv.reshape(B, T, NH_M, DV_M),
            ig, lf, mo, g_a, g_b)


def dsa_attend(q, q_idx, w_idx, t_pos, k, v, k_idx, top_k):
    L = k.shape[1]
    s_pos = jnp.arange(L, dtype=jnp.int32)
    rel = jax.nn.relu(jnp.einsum('bqhd,bsd->bqhs', q_idx, k_idx))
    score = jnp.einsum('bqhs,bqh->bqs', rel, w_idx).astype(jnp.float32) * IDX_SCALE
    adm = (s_pos[None, :] // CHUNK) <= (t_pos[:, None] // CHUNK)
    score = jnp.where(adm[None], score, -jnp.inf)
    _, sel = lax.top_k(score, top_k)
    gather = jax.vmap(lambda a, i: a[i])
    k_sel = gather(k, sel)
    v_sel = gather(v, sel)
    valid = (sel // CHUNK) <= (t_pos[None, :, None] // CHUNK)
    dist = jnp.abs(t_pos[None, :, None] - sel).astype(jnp.float32)
    logits = (jnp.einsum('bqhd,bqkd->bqhk', q, k_sel).astype(jnp.float32) * DH_A ** -0.5
              - alibi_slopes()[None, None, :, None] * dist[:, :, None, :])
    logits = jnp.where(valid[:, :, None, :], logits, -jnp.inf)
    p = jax.nn.softmax(logits, axis=-1).astype(v.dtype)
    return jnp.einsum('bqhk,bqkd->bqhd', p, v_sel)


def mlstm_heads(mq, mk, mv, ig, lf):
    f32 = jnp.float32
    q = jnp.moveaxis(mq, 2, 1).astype(f32) * DK_M ** -0.5
    k = jnp.moveaxis(mk, 2, 1).astype(f32)
    v = jnp.moveaxis(mv, 2, 1).astype(f32)
    return q, k, v, jnp.moveaxis(ig, 2, 1), jnp.moveaxis(lf, 2, 1)


def mlstm_chunk(carry, inp):
    C, n, m = carry
    q, k, v, ig, lf = inp
    L = q.shape[2]
    b = jnp.cumsum(lf, axis=-1)
    causal = jnp.tril(jnp.ones((L, L), dtype=bool))
    D = jnp.where(causal, b[..., :, None] - b[..., None, :] + ig[..., None, :], -jnp.inf)
    g = b + m[..., None]
    m_t = jnp.maximum(g, jnp.max(D, axis=-1))
    Dw = jnp.exp(D - m_t[..., None])
    gw = jnp.exp(g - m_t)
    S = jnp.einsum('bhtd,bhsd->bhts', q, k) * Dw
    num = jnp.einsum('bhts,bhsv->bhtv', S, v) + gw[..., None] * jnp.einsum('bhvd,bhtd->bhtv', C, q)
    nq = jnp.sum(S, axis=-1) + gw * jnp.einsum('bhd,bhtd->bht', n, q)
    h = num / jnp.maximum(jnp.abs(nq), jnp.exp(-m_t))[..., None]
    wL = Dw[..., -1, :]
    decay = gw[..., -1]
    C_new = decay[..., None, None] * C + jnp.einsum('bhs,bhsv,bhsd->bhvd', wL, v, k)
    n_new = decay[..., None] * n + jnp.einsum('bhs,bhsd->bhd', wL, k)
    return (C_new, n_new, m_t[..., -1]), h


def mix_output(attn, hm, mo, g_a, g_b, g_mnorm, w_ba, w_bb, w_out):
    B, H, T, dv = hm.shape
    hm = jnp.moveaxis(hm, 1, 2)
    hm = hm * lax.rsqrt(jnp.mean(hm * hm, axis=-1, keepdims=True) + EPS)
    hm = hm.reshape(B, T, H * dv).astype(mo.dtype) * g_mnorm * jax.nn.sigmoid(mo)
    y = jax.nn.sigmoid(g_a) * (attn @ w_ba) + jax.nn.sigmoid(g_b) * (hm @ w_bb)
    return y @ w_out


def token_mix_prompt(h, w_in, b_gates, g_mnorm, w_ba, w_bb, w_out):
    B, T, _ = h.shape
    aq, ak, av, iq, ik, iw, mq, mk, mv, ig, lf, mo, g_a, g_b = project(h, w_in, b_gates)
    pos = jnp.arange(T, dtype=jnp.int32)
    nb = T // QBLK
    top_k = min(TOPK_MAX, T // 4)

    def blk(a):
        return jnp.moveaxis(a.reshape((B, nb, QBLK) + a.shape[2:]), 1, 0)

    attn = lax.map(lambda xs: dsa_attend(*xs, ak, av, ik, top_k),
                   (blk(aq), blk(iq), blk(iw), pos.reshape(nb, QBLK)))
    attn = jnp.moveaxis(attn, 0, 1).reshape(B, T, N_HEADS_A * DH_A)
    q, k, v, igh, lfh = mlstm_heads(mq, mk, mv, ig, lf)
    nc = T // CHUNK

    def chk(a):
        return jnp.moveaxis(a.reshape(a.shape[:2] + (nc, CHUNK) + a.shape[3:]), 2, 0)

    init = (jnp.zeros((B, NH_M, DV_M, DK_M), jnp.float32), jnp.zeros((B, NH_M, DK_M), jnp.float32),
            jnp.zeros((B, NH_M), jnp.float32))
    (C, n, m), hs = lax.scan(mlstm_chunk, init, (chk(q), chk(k), chk(v), chk(igh), chk(lfh)))
    hm = jnp.moveaxis(hs, 0, 2).reshape(B, NH_M, T, DV_M)
    out = mix_output(attn, hm, mo, g_a, g_b, g_mnorm, w_ba, w_bb, w_out)
    return out, (ak, av, ik, C, n, m)


def token_mix_sample(h, cache_k, cache_v, cache_ik, s_C, s_n, s_m, w_in, b_gates, g_mnorm, w_ba, w_bb, w_out):
    B, T, _ = h.shape
    P = cache_k.shape[1]
    aq, ak, av, iq, ik, iw, mq, mk, mv, ig, lf, mo, g_a, g_b = project(h, w_in, b_gates)
    k_all = jnp.concatenate([cache_k, ak], axis=1)
    v_all = jnp.concatenate([cache_v, av], axis=1)
    ik_all = jnp.concatenate([cache_ik, ik], axis=1)
    t_pos = P + jnp.arange(T, dtype=jnp.int32)
    attn = dsa_attend(aq, iq, iw, t_pos, k_all, v_all, ik_all, min(TOPK_MAX, (P + T) // 4))
    attn = attn.reshape(B, T, N_HEADS_A * DH_A)
    q, k, v, igh, lfh = mlstm_heads(mq, mk, mv, ig, lf)
    carry = (s_C.astype(jnp.float32), s_n.astype(jnp.float32), s_m.astype(jnp.float32))
    (C, n, m), hm = mlstm_chunk(carry, (q, k, v, igh, lfh))
    out = mix_output(attn, hm, mo, g_a, g_b, g_mnorm, w_ba, w_bb, w_out)
    return out, (ak, av, ik, C, n, m)


def setup_inputs(seed: int = 0) -> dict:
    key = jax.random.key(seed)
    ks = jax.random.split(key, 24)
    nrm = jax.random.normal
    f32 = jnp.float32
    b_i = 0.1 * nrm(ks[10], (DEPTH, NH_M), f32)
    b_f = jnp.linspace(3.0, 6.0, NH_M, dtype=f32)[None, :] + 0.1 * nrm(ks[11], (DEPTH, NH_M), f32)
    return {
        'x_prompt': nrm(ks[0], (BATCH, SEQ, D_MODEL), f32),
        'x_sample': nrm(ks[1], (DEC_BATCH, DEC_SEQ, D_MODEL), f32),
        'cache_attn_k': nrm(ks[2], (DEPTH, DEC_BATCH, PAST_LEN, DH_A), f32),
        'cache_attn_v': nrm(ks[3], (DEPTH, DEC_BATCH, PAST_LEN, DH_A), f32),
        'cache_idx_k': nrm(ks[4], (DEPTH, DEC_BATCH, PAST_LEN, D_IDX), f32),
        'state_mlstm_C': 0.1 * nrm(ks[5], (DEPTH, DEC_BATCH, NH_M, DV_M, DK_M), f32),
        'state_mlstm_n': 0.1 * nrm(ks[6], (DEPTH, DEC_BATCH, NH_M, DK_M), f32),
        'state_mlstm_m': nrm(ks[7], (DEPTH, DEC_BATCH, NH_M), f32),
        'norm_ffn1': 1.0 + 0.01 * nrm(ks[8], (DEPTH, D_MODEL), f32),
        'w_ffn1_up': nrm(ks[9], (DEPTH, D_MODEL, 2 * D_FF), f32) * D_MODEL ** -0.5,
        'w_ffn1_down': nrm(ks[12], (DEPTH, D_FF, D_MODEL), f32) * D_FF ** -0.5,
        'norm_mix': 1.0 + 0.01 * nrm(ks[13], (DEPTH, D_MODEL), f32),
        'w_in': nrm(ks[14], (DEPTH, D_MODEL, D_IN), f32) * D_MODEL ** -0.5,
        'b_mlstm_gates': jnp.concatenate([b_i, b_f], axis=-1),
        'norm_mlstm_heads': 1.0 + 0.01 * nrm(ks[15], (DEPTH, NH_M * DV_M), f32),
        'w_branch_attn': nrm(ks[16], (DEPTH, N_HEADS_A * DH_A, D_MODEL), f32) * (N_HEADS_A * DH_A) ** -0.5,
        'w_branch_mlstm': nrm(ks[17], (DEPTH, NH_M * DV_M, D_MODEL), f32) * (NH_M * DV_M) ** -0.5,
        'w_out': nrm(ks[18], (DEPTH, D_MODEL, D_MODEL), f32) * D_MODEL ** -0.5,
        'norm_ffn2': 1.0 + 0.01 * nrm(ks[19], (DEPTH, D_MODEL), f32),
        'w_ffn2_up': nrm(ks[20], (DEPTH, D_MODEL, 2 * D_FF), f32) * D_MODEL ** -0.5,
        'w_ffn2_down': nrm(ks[21], (DEPTH, D_FF, D_MODEL), f32) * D_FF ** -0.5,
        'norm_final': 1.0 + 0.01 * nrm(ks[22], (D_MODEL,), f32),
    }


def reference(x_prompt, x_sample, cache_attn_k, cache_attn_v, cache_idx_k, state_mlstm_C, state_mlstm_n,
              state_mlstm_m, norm_ffn1, w_ffn1_up, w_ffn1_down, norm_mix, w_in, b_mlstm_gates, norm_mlstm_heads,
              w_branch_attn, w_branch_mlstm, w_out, norm_ffn2, w_ffn2_up, w_ffn2_down, norm_final):
    xp, xs = x_prompt, x_sample
    new_p = [[] for _ in range(6)]
    new_s = [[] for _ in range(6)]
    for l in range(DEPTH):
        xp = half_ffn(xp, norm_ffn1[l], w_ffn1_up[l], w_ffn1_down[l])
        xs = half_ffn(xs, norm_ffn1[l], w_ffn1_up[l], w_ffn1_down[l])
        mix_w = (w_in[l], b_mlstm_gates[l], norm_mlstm_heads[l], w_branch_attn[l], w_branch_mlstm[l], w_out[l])
        out_p, st_p = token_mix_prompt(rmsnorm(xp, norm_mix[l]), *mix_w)
        out_s, st_s = token_mix_sample(rmsnorm(xs, norm_mix[l]), cache_attn_k[l], cache_attn_v[l], cache_idx_k[l],
                                       state_mlstm_C[l], state_mlstm_n[l], state_mlstm_m[l], *mix_w)
        xp = xp + out_p
        xs = xs + out_s
        xp = half_ffn(xp, norm_ffn2[l], w_ffn2_up[l], w_ffn2_down[l])
        xs = half_ffn(xs, norm_ffn2[l], w_ffn2_up[l], w_ffn2_down[l])
        for i in range(6):
            new_p[i].append(st_p[i])
            new_s[i].append(st_s[i])
    y_prompt = rmsnorm(xp, norm_final)
    y_sample = rmsnorm(xs, norm_final)
    new_k_prompt = jnp.stack(new_p[0]).astype(cache_attn_k.dtype)
    new_v_prompt = jnp.stack(new_p[1]).astype(cache_attn_v.dtype)
    new_idxk_prompt = jnp.stack(new_p[2]).astype(cache_idx_k.dtype)
    new_C_prompt = jnp.stack(new_p[3]).astype(state_mlstm_C.dtype)
    new_n_prompt = jnp.stack(new_p[4]).astype(state_mlstm_n.dtype)
    new_m_prompt = jnp.stack(new_p[5]).astype(state_mlstm_m.dtype)
    new_k_sample = jnp.stack(new_s[0]).astype(cache_attn_k.dtype)
    new_v_sample = jnp.stack(new_s[1]).astype(cache_attn_v.dtype)
    new_idxk_sample = jnp.stack(new_s[2]).astype(cache_idx_k.dtype)
    new_C_sample = jnp.stack(new_s[3]).astype(state_mlstm_C.dtype)
    new_n_sample = jnp.stack(new_s[4]).astype(state_mlstm_n.dtype)
    new_m_sample = jnp.stack(new_s[5]).astype(state_mlstm_m.dtype)
    return (y_prompt, y_sample, new_k_prompt, new_v_prompt, new_idxk_prompt, new_C_prompt, new_n_prompt,
            new_m_prompt, new_k_sample, new_v_sample, new_idxk_sample, new_C_sample, new_n_sample, new_m_sample)
```

```python
import functools

import jax
import jax.numpy as jnp
from jax import lax
from jax.experimental import pallas as pl
from jax.experimental.pallas import tpu as pltpu

F32 = jnp.float32
BF16 = jnp.bfloat16

D_MODEL = 1024
CHUNK = 64
N_HEADS_A = 8
DH_A = 64
N_IDX = 4
D_IDX = 64
TOPK_MAX = 256
NH_M = 4
DK_M = 64
DV_M = 128
D_FF = 2816
EPS = 1e-6
IDX_SCALE = (N_IDX * D_IDX) ** -0.5
SPLITS = (N_HEADS_A * DH_A, DH_A, DH_A, N_IDX * D_IDX, D_IDX, N_IDX,
          NH_M * DK_M, NH_M * DK_M, NH_M * DV_M, 2 * NH_M, NH_M * DV_M, 2 * D_MODEL)

C_AQ = 0
C_KV = 512
C_IQ = 640
C_MISC = 896
C_MQ = 1024
C_MK = 1280
C_MV = 1536
C_MO = 2048
C_GA = 2560
C_GB = 3584
D_INR = 4608
MISC_IW = D_IDX
MISC_IG = D_IDX + N_IDX
MISC_FG = MISC_IG + NH_M

V7X_VMEM_LIMIT = 56 * 1024 * 1024
INT_MIN = -2 ** 31
KEY_NEG_INF = INT_MIN + 0x7FFFFF
NT_DIMS = (((1,), (1,)), ((), ()))
TN_DIMS = (((0,), (0,)), ((), ()))


def _rms(x, g):
    return x * lax.rsqrt(jnp.mean(x * x, axis=-1, keepdims=True) + EPS) * g


def _const_spec(shape):
    return pl.BlockSpec(shape, lambda *_: (0,) * len(shape), pipeline_mode=pl.Buffered(1))


FF_TILE = 256


def _ffn_kernel(x_ref, g_ref, wa_ref, wb_ref, wd_ref, gf_ref, o_ref, acc_ref, *, final_norm):
    x = x_ref[...]
    h = _rms(x, g_ref[...]).astype(BF16)
    for c in range(D_FF // FF_TILE):
        sl = slice(c * FF_TILE, (c + 1) * FF_TILE)
        a = jnp.dot(h, wa_ref[:, sl], preferred_element_type=F32)
        b = jnp.dot(h, wb_ref[:, sl], preferred_element_type=F32)
        u = (a * jax.nn.sigmoid(a) * b).astype(BF16)
        d = jnp.dot(u, wd_ref[sl, :], preferred_element_type=F32)
        if c == 0:
            acc_ref[...] = d
        else:
            acc_ref[...] += d
    y = x + 0.5 * acc_ref[...]
    if final_norm:
        y = _rms(y, gf_ref[...])
    o_ref[...] = y


def _ffn(x, g, wa, wb, wd, gf, *, final_norm, tm):
    n = x.shape[0]
    return pl.pallas_call(
        functools.partial(_ffn_kernel, final_norm=final_norm),
        grid=(n // tm,),
        in_specs=[
            pl.BlockSpec((tm, D_MODEL), lambda i: (i, 0)),
            _const_spec((1, D_MODEL)),
            _const_spec((D_MODEL, D_FF)),
            _const_spec((D_MODEL, D_FF)),
            _const_spec((D_FF, D_MODEL)),
            _const_spec((1, D_MODEL)),
        ],
        out_specs=pl.BlockSpec((tm, D_MODEL), lambda i: (i, 0)),
        out_shape=jax.ShapeDtypeStruct((n, D_MODEL), F32),
        scratch_shapes=[pltpu.VMEM((tm, D_MODEL), F32)],
        compiler_params=pltpu.CompilerParams(
            dimension_semantics=("parallel",), vmem_limit_bytes=V7X_VMEM_LIMIT),
        name="ffn_final" if final_norm else "ffn",
    )(x, g, wa, wb, wd, gf)


def _proj_kernel(x_ref, g_ref, w_ref, bias_ref, aq_ref, ak_ref, av_ref, iq_ref, ik_ref, misc_ref,
                 mq_ref, mk_ref, mv_ref, mo_ref, ga_ref, gb_ref):
    h = _rms(x_ref[...], g_ref[...]).astype(BF16)

    def cols(c0, width):
        return jnp.dot(h, w_ref[:, c0:c0 + width], preferred_element_type=F32)

    aq_ref[...] = cols(C_AQ, 512).astype(BF16)
    kv = cols(C_KV, 128)
    ak_ref[...] = kv[:, :DH_A]
    av_ref[...] = kv[:, DH_A:]
    iq_ref[...] = cols(C_IQ, 256).astype(BF16)
    misc = cols(C_MISC, 128) + bias_ref[...]
    misc_ref[...] = misc
    ik_ref[...] = misc[:, :D_IDX]
    mq_ref[...] = cols(C_MQ, 256).astype(BF16)
    mk_ref[...] = cols(C_MK, 256).astype(BF16)
    mv_ref[...] = cols(C_MV, 512).astype(BF16)
    mo_ref[...] = cols(C_MO, 512).astype(BF16)
    for c in range(4):
        ga_ref[:, c * 256:(c + 1) * 256] = cols(C_GA + c * 256, 256).astype(BF16)
        gb_ref[:, c * 256:(c + 1) * 256] = cols(C_GB + c * 256, 256).astype(BF16)


def _proj(x, g, w, bias, *, tm):
    n = x.shape[0]
    widths = [(512, BF16), (DH_A, F32), (DH_A, F32), (256, BF16), (D_IDX, F32), (128, F32),
              (256, BF16), (256, BF16), (512, BF16), (512, BF16), (D_MODEL, BF16), (D_MODEL, BF16)]
    return pl.pallas_call(
        _proj_kernel,
        grid=(n // tm,),
        in_specs=[
            pl.BlockSpec((tm, D_MODEL), lambda i: (i, 0)),
            _const_spec((1, D_MODEL)),
            _const_spec((D_MODEL, D_INR)),
            _const_spec((1, 128)),
        ],
        out_specs=[pl.BlockSpec((tm, w), lambda i: (i, 0)) for w, _ in widths],
        out_shape=[jax.ShapeDtypeStruct((n, w), dt) for w, dt in widths],
        compiler_params=pltpu.CompilerParams(
            dimension_semantics=("parallel",), vmem_limit_bytes=V7X_VMEM_LIMIT),
        name="proj",
    )(x, g, w, bias)


def _count(mask):
    return jnp.sum(jnp.where(mask, 1.0, 0.0), axis=1, keepdims=True)


def _dsa_kernel(aq_ref, iq_ref, misc_ref, k_ref, v_ref, ik_ref, o_ref, key_ref, p_ref,
                *, tq, n_keys, q_base, n_valid, top_k):
    q0 = q_base + pl.program_id(1) * tq
    ikb = ik_ref[0].astype(BF16)
    iw = misc_ref[:, MISC_IW:MISC_IW + N_IDX] * IDX_SCALE
    score = None
    for h in range(N_IDX):
        r = lax.dot_general(iq_ref[:, h * D_IDX:(h + 1) * D_IDX], ikb, NT_DIMS, preferred_element_type=F32)
        r = jnp.maximum(r, 0.0) * iw[:, h:h + 1]
        score = r if score is None else score + r

    row = lax.broadcasted_iota(jnp.int32, (tq, n_keys), 0) + q0
    col = lax.broadcasted_iota(jnp.int32, (tq, n_keys), 1)
    adm = (col >> 6) <= (row >> 6)
    if n_valid < n_keys:
        adm = adm & (col < n_valid)

    bits = pltpu.bitcast(score, jnp.int32)
    key = bits ^ ((bits >> 31) & 0x7FFFFFFF)
    key = jnp.where(key == -1, 0, key)
    key_ref[...] = jnp.where(adm, key, KEY_NEG_INF)

    kf = float(top_k)
    t0 = jnp.where(_count(key_ref[...] >= 0) >= kf, 0, INT_MIN).astype(jnp.int32)

    def bit_step(i, t):
        cand = t + lax.shift_left(jnp.int32(1), 30 - i)
        return jnp.where(_count(key_ref[...] >= cand) >= kf, cand, t)

    thr = lax.fori_loop(0, 31, bit_step, t0)

    keyv = key_ref[...]
    gt = keyv > thr
    eq = keyv == thr
    need = kf - _count(gt)
    p_ref[...] = jnp.full((tq, 1), n_keys, jnp.int32)

    @pl.when(jnp.max(_count(eq) - need) > 0.0)
    def _():
        p = jnp.zeros((tq, 1), jnp.int32)
        bit = 1 << ((n_keys - 1).bit_length() - 1)
        while bit:
            cand = p + bit
            c = _count(eq & (col < cand))
            p = jnp.where(c < need, cand, p)
            bit >>= 1
        p_ref[...] = p

    ninf = float("-inf")
    sel = jnp.where(gt, 0.0, jnp.where(eq, jnp.where(col <= p_ref[...], 0.0, ninf), ninf))
    maskbias = jnp.where(adm, sel, ninf)
    dist = jnp.abs(row - col).astype(F32)

    kb = k_ref[0].astype(BF16)
    vb = v_ref[0].astype(BF16)
    for h in range(N_HEADS_A):
        slope = 2.0 ** (-8.0 * (h + 1) / N_HEADS_A)
        logits = lax.dot_general(aq_ref[:, h * DH_A:(h + 1) * DH_A], kb, NT_DIMS, preferred_element_type=F32)
        logits = logits + (maskbias - slope * dist)
        m = jnp.max(logits, axis=1, keepdims=True)
        p = jnp.exp(logits - m)
        den = jnp.sum(p, axis=1, keepdims=True)
        o = jnp.dot(p.astype(BF16), vb, preferred_element_type=F32) / den
        o_ref[:, h * DH_A:(h + 1) * DH_A] = o.astype(BF16)


def _dsa(aq, iq, misc, k, v, ik, *, batch, q_len, tq, q_row0, n_q, n_keys, q_base, n_valid, top_k):
    nq_blk = n_q // tq
    per_b = q_len // tq
    blk0 = q_row0 // tq

    def qmap(b, i):
        return (b * per_b + blk0 + i, 0)

    kvspec = pl.BlockSpec((1, n_keys, DH_A), lambda b, i: (b, 0, 0))
    return pl.pallas_call(
        functools.partial(_dsa_kernel, tq=tq, n_keys=n_keys, q_base=q_base + q_row0, n_valid=n_valid, top_k=top_k),
        grid=(batch, nq_blk),
        in_specs=[
            pl.BlockSpec((tq, N_HEADS_A * DH_A), qmap),
            pl.BlockSpec((tq, N_IDX * D_IDX), qmap),
            pl.BlockSpec((tq, 128), qmap),
            kvspec, kvspec, kvspec,
        ],
        out_specs=pl.BlockSpec((tq, N_HEADS_A * DH_A), lambda b, i: (b * nq_blk + i, 0)),
        out_shape=jax.ShapeDtypeStruct((batch * n_q, N_HEADS_A * DH_A), BF16),
        scratch_shapes=[pltpu.VMEM((tq, n_keys), jnp.int32), pltpu.VMEM((tq, 1), jnp.int32)],
        compiler_params=pltpu.CompilerParams(
            dimension_semantics=("parallel", "arbitrary"), vmem_limit_bytes=V7X_VMEM_LIMIT),
        name="dsa",
    )(aq, iq, misc, k, v, ik)


def _mlstm_kernel(mq_ref, mk_ref, mv_ref, misc_ref, mo_ref, gn_ref, c0_ref, n0_ref, m0_ref,
                  o_ref, c_out, n_out, m_out, c_ref, n_ref, m_ref, *, lc):
    ci = pl.program_id(1)

    @pl.when(ci == 0)
    def _():
        c_ref[...] = c0_ref[0]
        n_ref[...] = n0_ref[0]
        m_ref[...] = m0_ref[0]

    misc = misc_ref[...]
    lane = lax.broadcasted_iota(jnp.int32, (8, 128), 1)
    sub = lax.broadcasted_iota(jnp.int32, (8, 128), 0)
    pick = jnp.where(lane == sub + MISC_IG, 1.0, 0.0)
    g_rows = lax.dot_general(pick, misc, NT_DIMS, preferred_element_type=F32, precision=lax.Precision.HIGHEST)
    g_cols = misc[:, MISC_IG:MISC_IG + 2 * NH_M]

    def log_sigmoid(x):
        return jnp.minimum(x, 0.0) - jnp.log1p(jnp.exp(-jnp.abs(x)))

    lf_rows = log_sigmoid(g_rows[NH_M:, :])
    lf_cols = log_sigmoid(g_cols[:, NH_M:])
    ti = lax.broadcasted_iota(jnp.int32, (lc, lc), 0)
    si = lax.broadcasted_iota(jnp.int32, (lc, lc), 1)
    causal = si <= ti
    tri = jnp.where(causal, 1.0, 0.0)
    b_cols = jnp.dot(tri, lf_cols, preferred_element_type=F32, precision=lax.Precision.HIGHEST)
    b_rows = lax.dot_general(lf_rows, tri, NT_DIMS, preferred_element_type=F32,
                             precision=lax.Precision.HIGHEST)

    ninf = float("-inf")
    for h in range(NH_M):
        q = mq_ref[:, h * DK_M:(h + 1) * DK_M]
        k = mk_ref[:, h * DK_M:(h + 1) * DK_M]
        v = mv_ref[:, h * DV_M:(h + 1) * DV_M]
        b_col = b_cols[:, h:h + 1]
        b_row = b_rows[h:h + 1, :]
        ig_row = g_rows[h:h + 1, :]
        ig_col = g_cols[:, h:h + 1]
        m_prev = m_ref[:, h:h + 1]
        c_prev = c_ref[h]
        n_prev = n_ref[h:h + 1, :]

        d = jnp.where(causal, b_col - b_row + ig_row, ninf)
        g = b_col + m_prev
        m_t = jnp.maximum(g, jnp.max(d, axis=1, keepdims=True))
        dw = jnp.exp(d - m_t)
        gw = jnp.exp(g - m_t)
        s = lax.dot_general(q, k, NT_DIMS, preferred_element_type=F32) * dw
        qc = lax.dot_general(q, c_prev.astype(BF16), NT_DIMS, preferred_element_type=F32)
        num = jnp.dot(s.astype(BF16), v, preferred_element_type=F32) + gw * qc
        qf = q.astype(F32)
        nq = jnp.sum(s, axis=1, keepdims=True) + gw * jnp.sum(qf * n_prev, axis=1, keepdims=True)
        hh = num / jnp.maximum(jnp.abs(nq), jnp.exp(-m_t))
        hh = hh * lax.rsqrt(jnp.mean(hh * hh, axis=1, keepdims=True) + EPS)
        hsl = slice(h * DV_M, (h + 1) * DV_M)
        gate = gn_ref[:, hsl] * jax.nn.sigmoid(mo_ref[:, hsl].astype(F32))
        o_ref[:, hsl] = (hh * gate).astype(BF16)

        b_last = b_col[lc - 1:lc, :]
        m_last = m_t[lc - 1:lc, :]
        decay = gw[lc - 1:lc, :]
        w_col = jnp.exp(b_last - b_col + ig_col - m_last)
        vw = (v.astype(F32) * w_col).astype(BF16)
        c_ref[h] = decay * c_prev + lax.dot_general(vw, k, TN_DIMS, preferred_element_type=F32)
        n_ref[h:h + 1, :] = decay * n_prev + jnp.sum(k.astype(F32) * w_col, axis=0, keepdims=True)
        m_ref[:, h:h + 1] = m_last

    @pl.when(ci == pl.num_programs(1) - 1)
    def _():
        c_out[0] = c_ref[...]
        n_out[0] = n_ref[...]
        m_out[0] = m_ref[...]


def _mlstm(mq, mk, mv, misc, mo, gn, c0, n0, m0, *, batch, seq, lc):
    nc = seq // lc

    def tmap(b, c):
        return (b * nc + c, 0)

    def smap4(b, c):
        return (b, 0, 0, 0)

    def smap3(b, c):
        return (b, 0, 0)

    return pl.pallas_call(
        functools.partial(_mlstm_kernel, lc=lc),
        grid=(batch, nc),
        in_specs=[
            pl.BlockSpec((lc, NH_M * DK_M), tmap),
            pl.BlockSpec((lc, NH_M * DK_M), tmap),
            pl.BlockSpec((lc, NH_M * DV_M), tmap),
            pl.BlockSpec((lc, 128), tmap),
            pl.BlockSpec((lc, NH_M * DV_M), tmap),
            pl.BlockSpec((1, NH_M * DV_M), lambda b, c: (0, 0)),
            pl.BlockSpec((1, NH_M, DV_M, DK_M), smap4),
            pl.BlockSpec((1, NH_M, DK_M), smap3),
            pl.BlockSpec((1, 1, NH_M), smap3),
        ],
        out_specs=[
            pl.BlockSpec((lc, NH_M * DV_M), tmap),
            pl.BlockSpec((1, NH_M, DV_M, DK_M), smap4),
            pl.BlockSpec((1, NH_M, DK_M), smap3),
            pl.BlockSpec((1, 1, NH_M), smap3),
        ],
        out_shape=[
            jax.ShapeDtypeStruct((batch * seq, NH_M * DV_M), BF16),
            jax.ShapeDtypeStruct((batch, NH_M, DV_M, DK_M), F32),
            jax.ShapeDtypeStruct((batch, NH_M, DK_M), F32),
            jax.ShapeDtypeStruct((batch, 1, NH_M), F32),
        ],
        scratch_shapes=[
            pltpu.VMEM((NH_M, DV_M, DK_M), F32),
            pltpu.VMEM((NH_M, DK_M), F32),
            pltpu.VMEM((1, NH_M), F32),
        ],
        compiler_params=pltpu.CompilerParams(
            dimension_semantics=("parallel", "arbitrary"), vmem_limit_bytes=V7X_VMEM_LIMIT),
        name="mlstm",
    )(mq, mk, mv, misc, mo, gn, c0, n0, m0)


def _mix_kernel(x_ref, attn_ref, hm_ref, ga_ref, gb_ref, wba_ref, wbb_ref, wo_ref, o_ref):
    a = jnp.dot(attn_ref[...], wba_ref[...], preferred_element_type=F32)
    b = jnp.dot(hm_ref[...], wbb_ref[...], preferred_element_type=F32)
    y = jax.nn.sigmoid(ga_ref[...].astype(F32)) * a + jax.nn.sigmoid(gb_ref[...].astype(F32)) * b
    o_ref[...] = x_ref[...] + jnp.dot(y.astype(BF16), wo_ref[...], preferred_element_type=F32)


def _mix(x, attn, hm, ga, gb, wba, wbb, wo, *, tm):
    n = x.shape[0]

    def tok(w):
        return pl.BlockSpec((tm, w), lambda i: (i, 0))

    return pl.pallas_call(
        _mix_kernel,
        grid=(n // tm,),
        in_specs=[tok(D_MODEL), tok(512), tok(512), tok(D_MODEL), tok(D_MODEL),
                  _const_spec((512, D_MODEL)), _const_spec((512, D_MODEL)), _const_spec((D_MODEL, D_MODEL))],
        out_specs=tok(D_MODEL),
        out_shape=jax.ShapeDtypeStruct((n, D_MODEL), F32),
        compiler_params=pltpu.CompilerParams(
            dimension_semantics=("parallel",), vmem_limit_bytes=V7X_VMEM_LIMIT),
        name="mix",
    )(x, attn, hm, ga, gb, wba, wbb, wo)


DSA_BUCKET = 512
DSA_TQ = 128
MLSTM_CHUNK = 256


def _regroup_w_in(w_in, b_gates):
    offs = [0]
    for s in SPLITS:
        offs.append(offs[-1] + s)
    aq, ak, av, iq, ik, iw, mq, mk, mv, mif, mo, gates = [w_in[:, offs[i]:offs[i + 1]] for i in range(len(SPLITS))]
    pad = jnp.zeros((D_MODEL, 128 - D_IDX - N_IDX - 2 * NH_M), w_in.dtype)
    w = jnp.concatenate([aq * DH_A ** -0.5, ak, av, iq, ik, iw, mif, pad, mq * DK_M ** -0.5, mk, mv, mo, gates],
                        axis=1)
    bias = jnp.zeros((1, 128), F32).at[0, MISC_IG:MISC_IG + 2 * NH_M].set(b_gates.astype(F32))
    return w.astype(BF16), bias


def _group(x, layer, *, batch, seq, tm, prompt, cache=None, state=None):
    n = batch * seq
    x = x.reshape(n, D_MODEL)
    x1 = _ffn(x, layer["g1"], *layer["ffn1"], layer["gf"], final_norm=False, tm=tm)
    aq, ak, av, iq, ik, misc, mq, mk, mv, mo, ga, gb = _proj(x1, layer["gmix"], layer["w_in"], layer["bias"], tm=tm)
    ak3, av3, ik3 = (a.reshape(batch, seq, DH_A) for a in (ak, av, ik))
    if prompt:
        top_k = min(TOPK_MAX, seq // 4)
        parts = []
        for j in range(seq // DSA_BUCKET):
            n_keys = (j + 1) * DSA_BUCKET
            parts.append(_dsa(aq, iq, misc, ak3, av3, ik3, batch=batch, q_len=seq, tq=DSA_TQ,
                              q_row0=j * DSA_BUCKET, n_q=DSA_BUCKET, n_keys=n_keys, q_base=0,
                              n_valid=n_keys, top_k=top_k).reshape(batch, DSA_BUCKET, -1))
        attn = jnp.concatenate(parts, axis=1).reshape(n, -1)
        c0 = jnp.zeros((batch, NH_M, DV_M, DK_M), F32)
        n0 = jnp.zeros((batch, NH_M, DK_M), F32)
        m0 = jnp.zeros((batch, 1, NH_M), F32)
        lc = MLSTM_CHUNK
    else:
        ck, cv, cik = cache
        past = ck.shape[1]
        total = past + seq
        n_keys = -(-total // 128) * 128
        zpad = jnp.zeros((batch, n_keys - total, DH_A), F32)
        k_all = jnp.concatenate([ck, ak3, zpad], axis=1)
        v_all = jnp.concatenate([cv, av3, zpad], axis=1)
        ik_all = jnp.concatenate([cik, ik3, zpad], axis=1)
        attn = _dsa(aq, iq, misc, k_all, v_all, ik_all, batch=batch, q_len=seq, tq=seq, q_row0=0, n_q=seq,
                    n_keys=n_keys, q_base=past, n_valid=total, top_k=min(TOPK_MAX, total // 4))
        c0, n0, m0 = state
        c0 = c0.astype(F32)
        n0 = n0.astype(F32)
        m0 = m0.astype(F32).reshape(batch, 1, NH_M)
        lc = seq
    hm, c_new, n_new, m_new = _mlstm(mq, mk, mv, misc, mo, layer["gn"], c0, n0, m0, batch=batch, seq=seq, lc=lc)
    x2 = _mix(x1, attn, hm, ga, gb, layer["wba"], layer["wbb"], layer["wo"], tm=tm)
    y = _ffn(x2, layer["g2"], *layer["ffn2"], layer["gf"], final_norm=True, tm=tm)
    return (y.reshape(batch, seq, D_MODEL), ak3, av3, ik3, c_new, n_new, m_new.reshape(batch, NH_M))


def kernel(x_prompt, x_sample, cache_attn_k, cache_attn_v, cache_idx_k, state_mlstm_C, state_mlstm_n,
           state_mlstm_m, norm_ffn1, w_ffn1_up, w_ffn1_down, norm_mix, w_in, b_mlstm_gates, norm_mlstm_heads,
           w_branch_attn, w_branch_mlstm, w_out, norm_ffn2, w_ffn2_up, w_ffn2_down, norm_final):
    assert w_in.shape[0] == 1, "single-layer trunk"
    bp, tp, _ = x_prompt.shape
    bs, ts, _ = x_sample.shape

    def ffn_w(w_up, w_down):
        return (w_up[0, :, :D_FF].astype(BF16), w_up[0, :, D_FF:].astype(BF16), w_down[0].astype(BF16))

    w_r, bias = _regroup_w_in(w_in[0], b_mlstm_gates[0])
    layer = dict(
        g1=norm_ffn1[0].reshape(1, -1), ffn1=ffn_w(w_ffn1_up, w_ffn1_down),
        g2=norm_ffn2[0].reshape(1, -1), ffn2=ffn_w(w_ffn2_up, w_ffn2_down),
        gmix=norm_mix[0].reshape(1, -1), w_in=w_r, bias=bias,
        gn=norm_mlstm_heads[0].reshape(1, -1).astype(F32),
        wba=w_branch_attn[0].astype(BF16), wbb=w_branch_mlstm[0].astype(BF16), wo=w_out[0].astype(BF16),
        gf=norm_final.reshape(1, -1),
    )
    yp, kp, vp, ikp, cp, np_, mp = _group(x_prompt, layer, batch=bp, seq=tp, tm=512, prompt=True)
    ys, ks, vs, iks, cs, ns, ms = _group(
        x_sample, layer, batch=bs, seq=ts, tm=bs * ts, prompt=False,
        cache=(cache_attn_k[0], cache_attn_v[0], cache_idx_k[0]),
        state=(state_mlstm_C[0], state_mlstm_n[0], state_mlstm_m[0]))
    dk, dv, di = cache_attn_k.dtype, cache_attn_v.dtype, cache_idx_k.dtype
    dc, dn, dm = state_mlstm_C.dtype, state_mlstm_n.dtype, state_mlstm_m.dtype
    return (yp, ys,
            kp[None].astype(dk), vp[None].astype(dv), ikp[None].astype(di),
            cp[None].astype(dc), np_[None].astype(dn), mp[None].astype(dm),
            ks[None].astype(dk), vs[None].astype(dv), iks[None].astype(di),
            cs[None].astype(dc), ns[None].astype(dn), ms[None].astype(dm))
```

```python
import functools

import jax
import jax.numpy as jnp
from jax import lax
from jax.experimental import pallas as pl
from jax.experimental.pallas import tpu as pltpu

F32 = jnp.float32
BF16 = jnp.bfloat16

D_MODEL = 1024
CHUNK = 64
N_HEADS_A = 8
DH_A = 64
N_IDX = 4
D_IDX = 64
TOPK_MAX = 256
NH_M = 4
DK_M = 64
DV_M = 128
D_FF = 2816
EPS = 1e-6
IDX_SCALE = (N_IDX * D_IDX) ** -0.5
SPLITS = (N_HEADS_A * DH_A, DH_A, DH_A, N_IDX * D_IDX, D_IDX, N_IDX,
          NH_M * DK_M, NH_M * DK_M, NH_M * DV_M, 2 * NH_M, NH_M * DV_M, 2 * D_MODEL)

C_AQ = 0
C_KV = 512
C_IQ = 640
C_MISC = 896
C_MQ = 1024
C_MK = 1280
C_MV = 1536
C_MO = 2048
C_GA = 2560
C_GB = 3584
D_INR = 4608
MISC_IW = D_IDX
MISC_IG = D_IDX + N_IDX
MISC_FG = MISC_IG + NH_M

V7X_VMEM_LIMIT = 56 * 1024 * 1024
INT_MIN = -2 ** 31
INT16_MIN = -2 ** 15
KEY_NEG_INF = INT_MIN + 0x7FFFFF
NT_DIMS = (((1,), (1,)), ((), ()))
TN_DIMS = (((0,), (0,)), ((), ()))


def _rms(x, g):
    return x * lax.rsqrt(jnp.mean(x * x, axis=-1, keepdims=True) + EPS) * g


def _const_spec(shape):
    return pl.BlockSpec(shape, lambda *_: (0,) * len(shape), pipeline_mode=pl.Buffered(1))


FF_TILE = 256


def _ffn_kernel(x_ref, g_ref, wa_ref, wb_ref, wd_ref, gf_ref, o_ref, acc_ref, *, final_norm):
    x = x_ref[...]
    h = _rms(x, g_ref[...]).astype(BF16)
    for c in range(D_FF // FF_TILE):
        sl = slice(c * FF_TILE, (c + 1) * FF_TILE)
        a = jnp.dot(h, wa_ref[:, sl], preferred_element_type=F32)
        b = jnp.dot(h, wb_ref[:, sl], preferred_element_type=F32)
        u = (a * jax.nn.sigmoid(a) * b).astype(BF16)
        d = jnp.dot(u, wd_ref[sl, :], preferred_element_type=F32)
        if c == 0:
            acc_ref[...] = d
        else:
            acc_ref[...] += d
    y = x + 0.5 * acc_ref[...]
    if final_norm:
        y = _rms(y, gf_ref[...])
    o_ref[...] = y


def _ffn(x, g, wa, wb, wd, gf, *, final_norm, tm):
    n = x.shape[0]
    return pl.pallas_call(
        functools.partial(_ffn_kernel, final_norm=final_norm),
        grid=(n // tm,),
        in_specs=[
            pl.BlockSpec((tm, D_MODEL), lambda i: (i, 0)),
            _const_spec((1, D_MODEL)),
            _const_spec((D_MODEL, D_FF)),
            _const_spec((D_MODEL, D_FF)),
            _const_spec((D_FF, D_MODEL)),
            _const_spec((1, D_MODEL)),
        ],
        out_specs=pl.BlockSpec((tm, D_MODEL), lambda i: (i, 0)),
        out_shape=jax.ShapeDtypeStruct((n, D_MODEL), F32),
        scratch_shapes=[pltpu.VMEM((tm, D_MODEL), F32)],
        compiler_params=pltpu.CompilerParams(
            dimension_semantics=("parallel",), vmem_limit_bytes=V7X_VMEM_LIMIT),
        name="ffn_final" if final_norm else "ffn",
    )(x, g, wa, wb, wd, gf)


def _proj_kernel(x_ref, g_ref, w_ref, bias_ref, aq_ref, ak_ref, av_ref, iq_ref, ik_ref, misc_ref,
                 mq_ref, mk_ref, mv_ref, mo_ref, ga_ref, gb_ref):
    h = _rms(x_ref[...], g_ref[...]).astype(BF16)

    def cols(c0, width):
        return jnp.dot(h, w_ref[:, c0:c0 + width], preferred_element_type=F32)

    aq_ref[...] = cols(C_AQ, 512).astype(BF16)
    kv = cols(C_KV, 128)
    ak_ref[...] = kv[:, :DH_A]
    av_ref[...] = kv[:, DH_A:]
    iq_ref[...] = cols(C_IQ, 256).astype(BF16)
    misc = cols(C_MISC, 128) + bias_ref[...]
    misc_ref[...] = misc
    ik_ref[...] = misc[:, :D_IDX]
    mq_ref[...] = cols(C_MQ, 256).astype(BF16)
    mk_ref[...] = cols(C_MK, 256).astype(BF16)
    mv_ref[...] = cols(C_MV, 512).astype(BF16)
    mo_ref[...] = cols(C_MO, 512).astype(BF16)
    for c in range(4):
        ga_ref[:, c * 256:(c + 1) * 256] = cols(C_GA + c * 256, 256).astype(BF16)
        gb_ref[:, c * 256:(c + 1) * 256] = cols(C_GB + c * 256, 256).astype(BF16)


def _proj(x, g, w, bias, *, tm):
    n = x.shape[0]
    widths = [(512, BF16), (DH_A, F32), (DH_A, F32), (256, BF16), (D_IDX, F32), (128, F32),
              (256, BF16), (256, BF16), (512, BF16), (512, BF16), (D_MODEL, BF16), (D_MODEL, BF16)]
    return pl.pallas_call(
        _proj_kernel,
        grid=(n // tm,),
        in_specs=[
            pl.BlockSpec((tm, D_MODEL), lambda i: (i, 0)),
            _const_spec((1, D_MODEL)),
            _const_spec((D_MODEL, D_INR)),
            _const_spec((1, 128)),
        ],
        out_specs=[pl.BlockSpec((tm, w), lambda i: (i, 0)) for w, _ in widths],
        out_shape=[jax.ShapeDtypeStruct((n, w), dt) for w, dt in widths],
        compiler_params=pltpu.CompilerParams(
            dimension_semantics=("parallel",), vmem_limit_bytes=V7X_VMEM_LIMIT),
        name="proj",
    )(x, g, w, bias)


def _count(mask):
    return jnp.sum(jnp.where(mask, 1.0, 0.0), axis=1, keepdims=True)


def _count16(ref, n_keys, pred):
    acc = None
    for c in range(n_keys // 128):
        m = jnp.where(pred(ref[:, c * 128:(c + 1) * 128]), jnp.int16(1), jnp.int16(0))
        acc = m if acc is None else acc + m
    return jnp.sum(acc.astype(F32), axis=1, keepdims=True)


def _kth_largest16(ref, need, n_keys):
    t = jnp.where(_count16(ref, n_keys, lambda blk: blk >= jnp.int16(0)) >= need, 0, INT16_MIN).astype(jnp.int32)
    for b in range(14, -1, -1):
        cand = t + (1 << b)
        cand16 = cand.astype(jnp.int16)
        t = jnp.where(_count16(ref, n_keys, lambda blk: blk >= cand16) >= need, cand, t)
    return t


def _dsa_kernel(aq_ref, iq_ref, misc_ref, k_ref, v_ref, ik_ref, o_ref, key_ref, hi_ref, lo_ref, p_ref,
                *, tq, n_keys, q_base, n_valid, top_k):
    q0 = q_base + pl.program_id(1) * tq
    ikb = ik_ref[0].astype(BF16)
    iw = misc_ref[:, MISC_IW:MISC_IW + N_IDX] * IDX_SCALE
    score = None
    for h in range(N_IDX):
        r = lax.dot_general(iq_ref[:, h * D_IDX:(h + 1) * D_IDX], ikb, NT_DIMS, preferred_element_type=F32)
        r = jnp.maximum(r, 0.0) * iw[:, h:h + 1]
        score = r if score is None else score + r

    row = lax.broadcasted_iota(jnp.int32, (tq, n_keys), 0) + q0
    col = lax.broadcasted_iota(jnp.int32, (tq, n_keys), 1)
    adm = (col >> 6) <= (row >> 6)
    if n_valid < n_keys:
        adm = adm & (col < n_valid)

    bits = pltpu.bitcast(score, jnp.int32)
    key = bits ^ ((bits >> 31) & 0x7FFFFFFF)
    key = jnp.where(key == -1, 0, key)
    key_ref[...] = jnp.where(adm, key, KEY_NEG_INF)

    hi_ref[...] = (key_ref[...] >> 16).astype(jnp.int16)
    lo_ref[...] = ((key_ref[...] & 0xFFFF) - 32768).astype(jnp.int16)
    kf = float(top_k)
    t_hi = _kth_largest16(hi_ref, kf, n_keys)
    t_hi16 = t_hi.astype(jnp.int16)
    need_lo = kf - _count16(hi_ref, n_keys, lambda blk: blk > t_hi16)
    lo_ref[...] = jnp.where(hi_ref[...] == t_hi16, lo_ref[...], jnp.int16(INT16_MIN))
    t_lo = _kth_largest16(lo_ref, need_lo, n_keys)
    thr = t_hi * 65536 + (t_lo + 32768)

    keyv = key_ref[...]
    gt = keyv > thr
    eq = keyv == thr
    need = kf - _count(gt)
    p_ref[...] = jnp.full((tq, 1), n_keys, jnp.int32)

    @pl.when(jnp.max(_count(eq) - need) > 0.0)
    def _():
        p = jnp.zeros((tq, 1), jnp.int32)
        bit = 1 << ((n_keys - 1).bit_length() - 1)
        while bit:
            cand = p + bit
            c = _count(eq & (col < cand))
            p = jnp.where(c < need, cand, p)
            bit >>= 1
        p_ref[...] = p

    ninf = float("-inf")
    sel = jnp.where(gt, 0.0, jnp.where(eq, jnp.where(col <= p_ref[...], 0.0, ninf), ninf))
    maskbias = jnp.where(adm, sel, ninf)
    dist = jnp.abs(row - col).astype(F32)

    kb = k_ref[0].astype(BF16)
    vb = v_ref[0].astype(BF16)
    for h in range(N_HEADS_A):
        slope = 2.0 ** (-8.0 * (h + 1) / N_HEADS_A)
        logits = lax.dot_general(aq_ref[:, h * DH_A:(h + 1) * DH_A], kb, NT_DIMS, preferred_element_type=F32)
        logits = logits + (maskbias - slope * dist)
        m = jnp.max(logits, axis=1, keepdims=True)
        p = jnp.exp(logits - m)
        den = jnp.sum(p, axis=1, keepdims=True)
        o = jnp.dot(p.astype(BF16), vb, preferred_element_type=F32) / den
        o_ref[:, h * DH_A:(h + 1) * DH_A] = o.astype(BF16)


def _dsa(aq, iq, misc, k, v, ik, *, batch, q_len, tq, q_row0, n_q, n_keys, q_base, n_valid, top_k):
    nq_blk = n_q // tq
    per_b = q_len // tq
    blk0 = q_row0 // tq

    def qmap(b, i):
        return (b * per_b + blk0 + i, 0)

    kvspec = pl.BlockSpec((1, n_keys, DH_A), lambda b, i: (b, 0, 0))
    return pl.pallas_call(
        functools.partial(_dsa_kernel, tq=tq, n_keys=n_keys, q_base=q_base + q_row0, n_valid=n_valid, top_k=top_k),
        grid=(batch, nq_blk),
        in_specs=[
            pl.BlockSpec((tq, N_HEADS_A * DH_A), qmap),
            pl.BlockSpec((tq, N_IDX * D_IDX), qmap),
            pl.BlockSpec((tq, 128), qmap),
            kvspec, kvspec, kvspec,
        ],
        out_specs=pl.BlockSpec((tq, N_HEADS_A * DH_A), lambda b, i: (b * nq_blk + i, 0)),
        out_shape=jax.ShapeDtypeStruct((batch * n_q, N_HEADS_A * DH_A), BF16),
        scratch_shapes=[pltpu.VMEM((tq, n_keys), jnp.int32), pltpu.VMEM((tq, n_keys), jnp.int16),
                        pltpu.VMEM((tq, n_keys), jnp.int16), pltpu.VMEM((tq, 1), jnp.int32)],
        compiler_params=pltpu.CompilerParams(
            dimension_semantics=("parallel", "arbitrary"), vmem_limit_bytes=V7X_VMEM_LIMIT),
        name="dsa",
    )(aq, iq, misc, k, v, ik)


def _mlstm_kernel(mq_ref, mk_ref, mv_ref, misc_ref, mo_ref, gn_ref, c0_ref, n0_ref, m0_ref,
                  o_ref, c_out, n_out, m_out, c_ref, n_ref, m_ref, *, lc):
    ci = pl.program_id(1)

    @pl.when(ci == 0)
    def _():
        c_ref[...] = c0_ref[0]
        n_ref[...] = n0_ref[0]
        m_ref[...] = m0_ref[0]

    misc = misc_ref[...]
    lane = lax.broadcasted_iota(jnp.int32, (8, 128), 1)
    sub = lax.broadcasted_iota(jnp.int32, (8, 128), 0)
    pick = jnp.where(lane == sub + MISC_IG, 1.0, 0.0)
    g_rows = lax.dot_general(pick, misc, NT_DIMS, preferred_element_type=F32, precision=lax.Precision.HIGHEST)
    g_cols = misc[:, MISC_IG:MISC_IG + 2 * NH_M]

    def log_sigmoid(x):
        return jnp.minimum(x, 0.0) - jnp.log1p(jnp.exp(-jnp.abs(x)))

    lf_rows = log_sigmoid(g_rows[NH_M:, :])
    lf_cols = log_sigmoid(g_cols[:, NH_M:])
    ti = lax.broadcasted_iota(jnp.int32, (lc, lc), 0)
    si = lax.broadcasted_iota(jnp.int32, (lc, lc), 1)
    causal = si <= ti
    tri = jnp.where(causal, 1.0, 0.0)
    b_cols = jnp.dot(tri, lf_cols, preferred_element_type=F32, precision=lax.Precision.HIGHEST)
    b_rows = lax.dot_general(lf_rows, tri, NT_DIMS, preferred_element_type=F32,
                             precision=lax.Precision.HIGHEST)

    ninf = float("-inf")
    for h in range(NH_M):
        q = mq_ref[:, h * DK_M:(h + 1) * DK_M]
        k = mk_ref[:, h * DK_M:(h + 1) * DK_M]
        v = mv_ref[:, h * DV_M:(h + 1) * DV_M]
        b_col = b_cols[:, h:h + 1]
        b_row = b_rows[h:h + 1, :]
        ig_row = g_rows[h:h + 1, :]
        ig_col = g_cols[:, h:h + 1]
        m_prev = m_ref[:, h:h + 1]
        c_prev = c_ref[h]
        n_prev = n_ref[h:h + 1, :]

        d = jnp.where(causal, b_col - b_row + ig_row, ninf)
        g = b_col + m_prev
        m_t = jnp.maximum(g, jnp.max(d, axis=1, keepdims=True))
        dw = jnp.exp(d - m_t)
        gw = jnp.exp(g - m_t)
        s = lax.dot_general(q, k, NT_DIMS, preferred_element_type=F32) * dw
        qc = lax.dot_general(q, c_prev.astype(BF16), NT_DIMS, preferred_element_type=F32)
        num = jnp.dot(s.astype(BF16), v, preferred_element_type=F32) + gw * qc
        qf = q.astype(F32)
        nq = jnp.sum(s, axis=1, keepdims=True) + gw * jnp.sum(qf * n_prev, axis=1, keepdims=True)
        hh = num / jnp.maximum(jnp.abs(nq), jnp.exp(-m_t))
        hh = hh * lax.rsqrt(jnp.mean(hh * hh, axis=1, keepdims=True) + EPS)
        hsl = slice(h * DV_M, (h + 1) * DV_M)
        gate = gn_ref[:, hsl] * jax.nn.sigmoid(mo_ref[:, hsl].astype(F32))
        o_ref[:, hsl] = (hh * gate).astype(BF16)

        b_last = b_col[lc - 1:lc, :]
        m_last = m_t[lc - 1:lc, :]
        decay = gw[lc - 1:lc, :]
        w_col = jnp.exp(b_last - b_col + ig_col - m_last)
        vw = (v.astype(F32) * w_col).astype(BF16)
        c_ref[h] = decay * c_prev + lax.dot_general(vw, k, TN_DIMS, preferred_element_type=F32)
        n_ref[h:h + 1, :] = decay * n_prev + jnp.sum(k.astype(F32) * w_col, axis=0, keepdims=True)
        m_ref[:, h:h + 1] = m_last

    @pl.when(ci == pl.num_programs(1) - 1)
    def _():
        c_out[0] = c_ref[...]
        n_out[0] = n_ref[...]
        m_out[0] = m_ref[...]


def _mlstm(mq, mk, mv, misc, mo, gn, c0, n0, m0, *, batch, seq, lc):
    nc = seq // lc

    def tmap(b, c):
        return (b * nc + c, 0)

    def smap4(b, c):
        return (b, 0, 0, 0)

    def smap3(b, c):
        return (b, 0, 0)

    return pl.pallas_call(
        functools.partial(_mlstm_kernel, lc=lc),
        grid=(batch, nc),
        in_specs=[
            pl.BlockSpec((lc, NH_M * DK_M), tmap),
            pl.BlockSpec((lc, NH_M * DK_M), tmap),
            pl.BlockSpec((lc, NH_M * DV_M), tmap),
            pl.BlockSpec((lc, 128), tmap),
            pl.BlockSpec((lc, NH_M * DV_M), tmap),
            pl.BlockSpec((1, NH_M * DV_M), lambda b, c: (0, 0)),
            pl.BlockSpec((1, NH_M, DV_M, DK_M), smap4),
            pl.BlockSpec((1, NH_M, DK_M), smap3),
            pl.BlockSpec((1, 1, NH_M), smap3),
        ],
        out_specs=[
            pl.BlockSpec((lc, NH_M * DV_M), tmap),
            pl.BlockSpec((1, NH_M, DV_M, DK_M), smap4),
            pl.BlockSpec((1, NH_M, DK_M), smap3),
            pl.BlockSpec((1, 1, NH_M), smap3),
        ],
        out_shape=[
            jax.ShapeDtypeStruct((batch * seq, NH_M * DV_M), BF16),
            jax.ShapeDtypeStruct((batch, NH_M, DV_M, DK_M), F32),
            jax.ShapeDtypeStruct((batch, NH_M, DK_M), F32),
            jax.ShapeDtypeStruct((batch, 1, NH_M), F32),
        ],
        scratch_shapes=[
            pltpu.VMEM((NH_M, DV_M, DK_M), F32),
            pltpu.VMEM((NH_M, DK_M), F32),
            pltpu.VMEM((1, NH_M), F32),
        ],
        compiler_params=pltpu.CompilerParams(
            dimension_semantics=("parallel", "arbitrary"), vmem_limit_bytes=V7X_VMEM_LIMIT),
        name="mlstm",
    )(mq, mk, mv, misc, mo, gn, c0, n0, m0)


def _mix_kernel(x_ref, attn_ref, hm_ref, ga_ref, gb_ref, wba_ref, wbb_ref, wo_ref, o_ref):
    a = jnp.dot(attn_ref[...], wba_ref[...], preferred_element_type=F32)
    b = jnp.dot(hm_ref[...], wbb_ref[...], preferred_element_type=F32)
    y = jax.nn.sigmoid(ga_ref[...].astype(F32)) * a + jax.nn.sigmoid(gb_ref[...].astype(F32)) * b
    o_ref[...] = x_ref[...] + jnp.dot(y.astype(BF16), wo_ref[...], preferred_element_type=F32)


def _mix(x, attn, hm, ga, gb, wba, wbb, wo, *, tm):
    n = x.shape[0]

    def tok(w):
        return pl.BlockSpec((tm, w), lambda i: (i, 0))

    return pl.pallas_call(
        _mix_kernel,
        grid=(n // tm,),
        in_specs=[tok(D_MODEL), tok(512), tok(512), tok(D_MODEL), tok(D_MODEL),
                  _const_spec((512, D_MODEL)), _const_spec((512, D_MODEL)), _const_spec((D_MODEL, D_MODEL))],
        out_specs=tok(D_MODEL),
        out_shape=jax.ShapeDtypeStruct((n, D_MODEL), F32),
        compiler_params=pltpu.CompilerParams(
            dimension_semantics=("parallel",), vmem_limit_bytes=V7X_VMEM_LIMIT),
        name="mix",
    )(x, attn, hm, ga, gb, wba, wbb, wo)


DSA_BUCKET = 256
DSA_TQ = 256
MLSTM_CHUNK = 256


def _regroup_w_in(w_in, b_gates):
    offs = [0]
    for s in SPLITS:
        offs.append(offs[-1] + s)
    aq, ak, av, iq, ik, iw, mq, mk, mv, mif, mo, gates = [w_in[:, offs[i]:offs[i + 1]] for i in range(len(SPLITS))]
    pad = jnp.zeros((D_MODEL, 128 - D_IDX - N_IDX - 2 * NH_M), w_in.dtype)
    w = jnp.concatenate([aq * DH_A ** -0.5, ak, av, iq, ik, iw, mif, pad, mq * DK_M ** -0.5, mk, mv, mo, gates],
                        axis=1)
    bias = jnp.zeros((1, 128), F32).at[0, MISC_IG:MISC_IG + 2 * NH_M].set(b_gates.astype(F32))
    return w.astype(BF16), bias


def _group(x, layer, *, batch, seq, tm, prompt, cache=None, state=None):
    n = batch * seq
    x = x.reshape(n, D_MODEL)
    x1 = _ffn(x, layer["g1"], *layer["ffn1"], layer["gf"], final_norm=False, tm=tm)
    aq, ak, av, iq, ik, misc, mq, mk, mv, mo, ga, gb = _proj(x1, layer["gmix"], layer["w_in"], layer["bias"], tm=tm)
    ak3, av3, ik3 = (a.reshape(batch, seq, DH_A) for a in (ak, av, ik))
    if prompt:
        top_k = min(TOPK_MAX, seq // 4)
        parts = []
        for j in range(seq // DSA_BUCKET):
            n_keys = (j + 1) * DSA_BUCKET
            parts.append(_dsa(aq, iq, misc, ak3, av3, ik3, batch=batch, q_len=seq, tq=DSA_TQ,
                              q_row0=j * DSA_BUCKET, n_q=DSA_BUCKET, n_keys=n_keys, q_base=0,
                              n_valid=n_keys, top_k=top_k).reshape(batch, DSA_BUCKET, -1))
        attn = jnp.concatenate(parts, axis=1).reshape(n, -1)
        c0 = jnp.zeros((batch, NH_M, DV_M, DK_M), F32)
        n0 = jnp.zeros((batch, NH_M, DK_M), F32)
        m0 = jnp.zeros((batch, 1, NH_M), F32)
        lc = MLSTM_CHUNK
    else:
        ck, cv, cik = cache
        past = ck.shape[1]
        total = past + seq
        n_keys = -(-total // 128) * 128
        zpad = jnp.zeros((batch, n_keys - total, DH_A), F32)
        k_all = jnp.concatenate([ck, ak3, zpad], axis=1)
        v_all = jnp.concatenate([cv, av3, zpad], axis=1)
        ik_all = jnp.concatenate([cik, ik3, zpad], axis=1)
        attn = _dsa(aq, iq, misc, k_all, v_all, ik_all, batch=batch, q_len=seq, tq=seq, q_row0=0, n_q=seq,
                    n_keys=n_keys, q_base=past, n_valid=total, top_k=min(TOPK_MAX, total // 4))
        c0, n0, m0 = state
        c0 = c0.astype(F32)
        n0 = n0.astype(F32)
        m0 = m0.astype(F32).reshape(batch, 1, NH_M)
        lc = seq
    hm, c_new, n_new, m_new = _mlstm(mq, mk, mv, misc, mo, layer["gn"], c0, n0, m0, batch=batch, seq=seq, lc=lc)
    x2 = _mix(x1, attn, hm, ga, gb, layer["wba"], layer["wbb"], layer["wo"], tm=tm)
    y = _ffn(x2, layer["g2"], *layer["ffn2"], layer["gf"], final_norm=True, tm=tm)
    return (y.reshape(batch, seq, D_MODEL), ak3, av3, ik3, c_new, n_new, m_new.reshape(batch, NH_M))


def kernel(x_prompt, x_sample, cache_attn_k, cache_attn_v, cache_idx_k, state_mlstm_C, state_mlstm_n,
           state_mlstm_m, norm_ffn1, w_ffn1_up, w_ffn1_down, norm_mix, w_in, b_mlstm_gates, norm_mlstm_heads,
           w_branch_attn, w_branch_mlstm, w_out, norm_ffn2, w_ffn2_up, w_ffn2_down, norm_final):
    assert w_in.shape[0] == 1, "single-layer trunk"
    bp, tp, _ = x_prompt.shape
    bs, ts, _ = x_sample.shape

    def ffn_w(w_up, w_down):
        return (w_up[0, :, :D_FF].astype(BF16), w_up[0, :, D_FF:].astype(BF16), w_down[0].astype(BF16))

    w_r, bias = _regroup_w_in(w_in[0], b_mlstm_gates[0])
    layer = dict(
        g1=norm_ffn1[0].reshape(1, -1), ffn1=ffn_w(w_ffn1_up, w_ffn1_down),
        g2=norm_ffn2[0].reshape(1, -1), ffn2=ffn_w(w_ffn2_up, w_ffn2_down),
        gmix=norm_mix[0].reshape(1, -1), w_in=w_r, bias=bias,
        gn=norm_mlstm_heads[0].reshape(1, -1).astype(F32),
        wba=w_branch_attn[0].astype(BF16), wbb=w_branch_mlstm[0].astype(BF16), wo=w_out[0].astype(BF16),
        gf=norm_final.reshape(1, -1),
    )
    yp, kp, vp, ikp, cp, np_, mp = _group(x_prompt, layer, batch=bp, seq=tp, tm=512, prompt=True)
    ys, ks, vs, iks, cs, ns, ms = _group(
        x_sample, layer, batch=bs, seq=ts, tm=bs * ts, prompt=False,
        cache=(cache_attn_k[0], cache_attn_v[0], cache_idx_k[0]),
        state=(state_mlstm_C[0], state_mlstm_n[0], state_mlstm_m[0]))
    dk, dv, di = cache_attn_k.dtype, cache_attn_v.dtype, cache_idx_k.dtype
    dc, dn, dm = state_mlstm_C.dtype, state_mlstm_n.dtype, state_mlstm_m.dtype
    return (yp, ys,
            kp[None].astype(dk), vp[None].astype(dv), ikp[None].astype(di),
            cp[None].astype(dc), np_[None].astype(dn), mp[None].astype(dm),
            ks[None].astype(dk), vs[None].astype(dv), iks[None].astype(di),
            cs[None].astype(dc), ns[None].astype(dn), ms[None].astype(dm))
```

```python
import functools

import jax
import jax.numpy as jnp
from jax import lax
from jax.experimental import pallas as pl
from jax.experimental.pallas import tpu as pltpu

F32 = jnp.float32
BF16 = jnp.bfloat16

D_MODEL = 1024
CHUNK = 64
N_HEADS_A = 8
DH_A = 64
N_IDX = 4
D_IDX = 64
TOPK_MAX = 256
NH_M = 4
DK_M = 64
DV_M = 128
D_FF = 2816
EPS = 1e-6
IDX_SCALE = (N_IDX * D_IDX) ** -0.5
SPLITS = (N_HEADS_A * DH_A, DH_A, DH_A, N_IDX * D_IDX, D_IDX, N_IDX,
          NH_M * DK_M, NH_M * DK_M, NH_M * DV_M, 2 * NH_M, NH_M * DV_M, 2 * D_MODEL)

C_AQ = 0
C_KV = 512
C_IQ = 640
C_MISC = 896
C_MQ = 1024
C_MK = 1280
C_MV = 1536
C_MO = 2048
C_GA = 2560
C_GB = 3584
D_INR = 4608
MISC_IW = D_IDX
MISC_IG = D_IDX + N_IDX
MISC_FG = MISC_IG + NH_M

V7X_VMEM_LIMIT = 56 * 1024 * 1024
LOG2E = 1.4426950408889634
INT_MIN = -2 ** 31
INT16_MIN = -2 ** 15
KEY_NEG_INF = INT_MIN + 0x7FFFFF
NT_DIMS = (((1,), (1,)), ((), ()))
TN_DIMS = (((0,), (0,)), ((), ()))


def _rms(x, g):
    return x * lax.rsqrt(jnp.mean(x * x, axis=-1, keepdims=True) + EPS) * g


def _const_spec(shape):
    return pl.BlockSpec(shape, lambda *_: (0,) * len(shape), pipeline_mode=pl.Buffered(1))


FF_TILE = 256


def _ffn_kernel(x_ref, g_ref, wa_ref, wb_ref, wd_ref, gf_ref, o_ref, acc_ref, *, final_norm):
    x = x_ref[...]
    h = _rms(x, g_ref[...]).astype(BF16)
    for c in range(D_FF // FF_TILE):
        sl = slice(c * FF_TILE, (c + 1) * FF_TILE)
        a = jnp.dot(h, wa_ref[:, sl], preferred_element_type=F32)
        b = jnp.dot(h, wb_ref[:, sl], preferred_element_type=F32)
        u = (a * jax.nn.sigmoid(a) * b).astype(BF16)
        d = jnp.dot(u, wd_ref[sl, :], preferred_element_type=F32)
        if c == 0:
            acc_ref[...] = d
        else:
            acc_ref[...] += d
    y = x + 0.5 * acc_ref[...]
    if final_norm:
        y = _rms(y, gf_ref[...])
    o_ref[...] = y


def _ffn(x, g, wa, wb, wd, gf, *, final_norm, tm):
    n = x.shape[0]
    return pl.pallas_call(
        functools.partial(_ffn_kernel, final_norm=final_norm),
        grid=(n // tm,),
        in_specs=[
            pl.BlockSpec((tm, D_MODEL), lambda i: (i, 0)),
            _const_spec((1, D_MODEL)),
            _const_spec((D_MODEL, D_FF)),
            _const_spec((D_MODEL, D_FF)),
            _const_spec((D_FF, D_MODEL)),
            _const_spec((1, D_MODEL)),
        ],
        out_specs=pl.BlockSpec((tm, D_MODEL), lambda i: (i, 0)),
        out_shape=jax.ShapeDtypeStruct((n, D_MODEL), F32),
        scratch_shapes=[pltpu.VMEM((tm, D_MODEL), F32)],
        compiler_params=pltpu.CompilerParams(
            dimension_semantics=("parallel",), vmem_limit_bytes=V7X_VMEM_LIMIT),
        name="ffn_final" if final_norm else "ffn",
    )(x, g, wa, wb, wd, gf)


def _proj_kernel(x_ref, g_ref, w_ref, bias_ref, aq_ref, ak_ref, av_ref, iq_ref, ik_ref, misc_ref,
                 mq_ref, mk_ref, mv_ref, mo_ref, ga_ref, gb_ref):
    h = _rms(x_ref[...], g_ref[...]).astype(BF16)

    def cols(c0, width):
        return jnp.dot(h, w_ref[:, c0:c0 + width], preferred_element_type=F32)

    aq_ref[...] = cols(C_AQ, 512).astype(BF16)
    kv = cols(C_KV, 128)
    ak_ref[...] = kv[:, :DH_A]
    av_ref[...] = kv[:, DH_A:]
    iq_ref[...] = cols(C_IQ, 256).astype(BF16)
    misc = cols(C_MISC, 128) + bias_ref[...]
    misc_ref[...] = misc
    ik_ref[...] = misc[:, :D_IDX]
    mq_ref[...] = cols(C_MQ, 256).astype(BF16)
    mk_ref[...] = cols(C_MK, 256).astype(BF16)
    mv_ref[...] = cols(C_MV, 512).astype(BF16)
    mo_ref[...] = cols(C_MO, 512).astype(BF16)
    for c in range(4):
        ga_ref[:, c * 256:(c + 1) * 256] = cols(C_GA + c * 256, 256).astype(BF16)
        gb_ref[:, c * 256:(c + 1) * 256] = cols(C_GB + c * 256, 256).astype(BF16)


def _proj(x, g, w, bias, *, tm):
    n = x.shape[0]
    widths = [(512, BF16), (DH_A, F32), (DH_A, F32), (256, BF16), (D_IDX, F32), (128, F32),
              (256, BF16), (256, BF16), (512, BF16), (512, BF16), (D_MODEL, BF16), (D_MODEL, BF16)]
    return pl.pallas_call(
        _proj_kernel,
        grid=(n // tm,),
        in_specs=[
            pl.BlockSpec((tm, D_MODEL), lambda i: (i, 0)),
            _const_spec((1, D_MODEL)),
            _const_spec((D_MODEL, D_INR)),
            _const_spec((1, 128)),
        ],
        out_specs=[pl.BlockSpec((tm, w), lambda i: (i, 0)) for w, _ in widths],
        out_shape=[jax.ShapeDtypeStruct((n, w), dt) for w, dt in widths],
        compiler_params=pltpu.CompilerParams(
            dimension_semantics=("parallel",), vmem_limit_bytes=V7X_VMEM_LIMIT),
        name="proj",
    )(x, g, w, bias)


def _count(mask):
    return jnp.sum(jnp.where(mask, 1.0, 0.0), axis=1, keepdims=True)


def _count16(ref, n_keys, pred):
    acc = None
    for c in range(n_keys // 128):
        m = jnp.where(pred(ref[:, c * 128:(c + 1) * 128]), jnp.int16(1), jnp.int16(0))
        acc = m if acc is None else acc + m
    return jnp.sum(acc.astype(F32), axis=1, keepdims=True)


def _kth_largest16(ref, need, n_keys):
    t = jnp.where(_count16(ref, n_keys, lambda blk: blk >= jnp.int16(0)) >= need, 0, INT16_MIN).astype(jnp.int32)
    for b in range(14, -1, -1):
        cand = t + (1 << b)
        cand16 = cand.astype(jnp.int16)
        t = jnp.where(_count16(ref, n_keys, lambda blk: blk >= cand16) >= need, cand, t)
    return t


def _dsa_kernel(aq_ref, iq_ref, misc_ref, k_ref, v_ref, ik_ref, o_ref, key_ref, hi_ref, lo_ref, p_ref,
                *, tq, n_keys, q_base, n_valid, top_k):
    q0 = q_base + pl.program_id(1) * tq
    ikb = ik_ref[0].astype(BF16)
    iw = misc_ref[:, MISC_IW:MISC_IW + N_IDX] * IDX_SCALE
    score = None
    for h in range(N_IDX):
        r = lax.dot_general(iq_ref[:, h * D_IDX:(h + 1) * D_IDX], ikb, NT_DIMS, preferred_element_type=F32)
        r = jnp.maximum(r, 0.0) * iw[:, h:h + 1]
        score = r if score is None else score + r

    row = lax.broadcasted_iota(jnp.int32, (tq, n_keys), 0) + q0
    col = lax.broadcasted_iota(jnp.int32, (tq, n_keys), 1)
    adm = (col >> 6) <= (row >> 6)
    if n_valid < n_keys:
        adm = adm & (col < n_valid)

    bits = pltpu.bitcast(score, jnp.int32)
    key = bits ^ ((bits >> 31) & 0x7FFFFFFF)
    key = jnp.where(key == -1, 0, key)
    key_ref[...] = jnp.where(adm, key, KEY_NEG_INF)

    hi_ref[...] = (key_ref[...] >> 16).astype(jnp.int16)
    lo_ref[...] = ((key_ref[...] & 0xFFFF) - 32768).astype(jnp.int16)
    kf = float(top_k)
    t_hi = _kth_largest16(hi_ref, kf, n_keys)
    t_hi16 = t_hi.astype(jnp.int16)
    need_lo = kf - _count16(hi_ref, n_keys, lambda blk: blk > t_hi16)
    lo_ref[...] = jnp.where(hi_ref[...] == t_hi16, lo_ref[...], jnp.int16(INT16_MIN))
    t_lo = _kth_largest16(lo_ref, need_lo, n_keys)
    thr = t_hi * 65536 + (t_lo + 32768)

    keyv = key_ref[...]
    gt = keyv > thr
    eq = keyv == thr
    need = kf - _count(gt)
    p_ref[...] = jnp.full((tq, 1), n_keys, jnp.int32)

    @pl.when(jnp.max(_count(eq) - need) > 0.0)
    def _():
        p = jnp.zeros((tq, 1), jnp.int32)
        bit = 1 << ((n_keys - 1).bit_length() - 1)
        while bit:
            cand = p + bit
            c = _count(eq & (col < cand))
            p = jnp.where(c < need, cand, p)
            bit >>= 1
        p_ref[...] = p

    ninf = float("-inf")
    sel = jnp.where(gt, 0.0, jnp.where(eq, jnp.where(col <= p_ref[...], 0.0, ninf), ninf))
    maskbias = jnp.where(adm, sel, ninf)
    dist = jnp.abs(row - col).astype(F32)

    kb = k_ref[0].astype(BF16)
    vb = v_ref[0].astype(BF16)
    for h in range(N_HEADS_A):
        slope = LOG2E * 2.0 ** (-8.0 * (h + 1) / N_HEADS_A)
        logits = lax.dot_general(aq_ref[:, h * DH_A:(h + 1) * DH_A], kb, NT_DIMS, preferred_element_type=F32)
        logits = logits + (maskbias - slope * dist)
        m = jnp.max(logits, axis=1, keepdims=True)
        p = jnp.exp2(logits - m)
        den = jnp.sum(p, axis=1, keepdims=True)
        o = jnp.dot(p.astype(BF16), vb, preferred_element_type=F32) / den
        o_ref[:, h * DH_A:(h + 1) * DH_A] = o.astype(BF16)


def _dsa(aq, iq, misc, k, v, ik, *, batch, q_len, tq, q_row0, n_q, n_keys, q_base, n_valid, top_k):
    nq_blk = n_q // tq
    per_b = q_len // tq
    blk0 = q_row0 // tq

    def qmap(b, i):
        return (b * per_b + blk0 + i, 0)

    kvspec = pl.BlockSpec((1, n_keys, DH_A), lambda b, i: (b, 0, 0))
    return pl.pallas_call(
        functools.partial(_dsa_kernel, tq=tq, n_keys=n_keys, q_base=q_base + q_row0, n_valid=n_valid, top_k=top_k),
        grid=(batch, nq_blk),
        in_specs=[
            pl.BlockSpec((tq, N_HEADS_A * DH_A), qmap),
            pl.BlockSpec((tq, N_IDX * D_IDX), qmap),
            pl.BlockSpec((tq, 128), qmap),
            kvspec, kvspec, kvspec,
        ],
        out_specs=pl.BlockSpec((tq, N_HEADS_A * DH_A), lambda b, i: (b * nq_blk + i, 0)),
        out_shape=jax.ShapeDtypeStruct((batch * n_q, N_HEADS_A * DH_A), BF16),
        scratch_shapes=[pltpu.VMEM((tq, n_keys), jnp.int32), pltpu.VMEM((tq, n_keys), jnp.int16),
                        pltpu.VMEM((tq, n_keys), jnp.int16), pltpu.VMEM((tq, 1), jnp.int32)],
        compiler_params=pltpu.CompilerParams(
            dimension_semantics=("parallel", "arbitrary"), vmem_limit_bytes=V7X_VMEM_LIMIT),
        name="dsa",
    )(aq, iq, misc, k, v, ik)


KB = 128
N_ACC = 4
QX_ALIBI = DH_A


def _fold16(x, op):
    parts = [x[r:r + 16, :] for r in range(0, x.shape[0], 16)]
    while len(parts) > 1:
        parts = [op(parts[i], parts[i + 1]) for i in range(0, len(parts), 2)]
    return parts[0]


def _flag16(mask):
    return jnp.where(mask, jnp.int16(1), jnp.int16(0))


def _sum16_t(n_keys, flags):
    accs = [None] * N_ACC
    for i, r in enumerate(range(0, n_keys, KB)):
        m = _fold16(flags(r), jnp.add)
        j = i % N_ACC
        accs[j] = m if accs[j] is None else accs[j] + m
    accs = [a for a in accs if a is not None]
    tot = accs[0]
    for a in accs[1:]:
        tot = tot + a
    return jnp.sum(tot.astype(jnp.int32).astype(F32), axis=0, keepdims=True)


def _kth_largest16_t(ref, need, n_keys, total):
    c = _sum16_t(n_keys, lambda r: _flag16(ref[r:r + KB, :] >= jnp.int16(0)))
    ok = c >= need
    t = jnp.where(ok, 0, INT16_MIN).astype(jnp.int32)
    cnt = jnp.where(ok, c, total)
    for b in range(14, -1, -1):
        cand = t + (1 << b)
        cand16 = cand.astype(jnp.int16)
        c = _sum16_t(n_keys, lambda r, cand16=cand16: _flag16(ref[r:r + KB, :] >= cand16))
        ok = c >= need
        t = jnp.where(ok, cand, t)
        cnt = jnp.where(ok, c, cnt)
    return t, cnt


def _dsa_t_kernel(aq_ref, iq_ref, misc_ref, acoef_ref, k_ref, v_ref, ik_ref, o_ref,
                  kx_ref, vx_ref, vt_ref, qx_ref, iqh_ref, hi_ref, lo_ref, mb_ref, fix_ref, pt_ref, ot_ref, p_ref,
                  *, tq, n_keys, top_k):
    q0 = n_keys - tq
    n_blk = n_keys // KB
    ninf = float("-inf")

    kx_ref[:, :DH_A] = k_ref[0].astype(BF16)
    s_pos = lax.broadcasted_iota(jnp.int32, (n_keys, DH_A), 0)
    ln = lax.broadcasted_iota(jnp.int32, (n_keys, DH_A), 1)
    s_part = jnp.where((ln & 1) == 0, s_pos >> 6, s_pos & (CHUNK - 1))
    kx_ref[:, DH_A:] = jnp.where(ln < 4, s_part, 0).astype(F32).astype(BF16)
    vx_ref[:, :DH_A] = v_ref[0]
    vx_ref[:, DH_A:] = jnp.where(ln == 0, 1.0, 0.0)
    vt_ref[...] = vx_ref[...].T.astype(BF16)
    for h in range(N_HEADS_A):
        qx_ref[h, :, :DH_A] = aq_ref[:, h * DH_A:(h + 1) * DH_A]
        qx_ref[h, :, DH_A:] = jnp.broadcast_to(acoef_ref[h:h + 1, DH_A:], (tq, DH_A)).astype(BF16)

    lane8 = lax.broadcasted_iota(jnp.int32, (8, 128), 1)
    sub8 = lax.broadcasted_iota(jnp.int32, (8, 128), 0)
    pick = jnp.where(lane8 == sub8 + MISC_IW, 1.0, 0.0)
    iw_rows = lax.dot_general(pick, misc_ref[...], NT_DIMS, preferred_element_type=F32,
                              precision=lax.Precision.HIGHEST) * IDX_SCALE
    for h in range(N_IDX):
        iqh_ref[h] = iq_ref[:, h * D_IDX:(h + 1) * D_IDX]
    t_pos = q0 + lax.broadcasted_iota(jnp.int32, (KB, tq), 1)
    for i in range(n_blk):
        r = i * KB
        ikb = ik_ref[0, r:r + KB, :].astype(BF16)
        score = None
        for h in range(N_IDX):
            rel = lax.dot_general(ikb, iqh_ref[h], NT_DIMS, preferred_element_type=F32)
            rel = jnp.maximum(rel, 0.0) * iw_rows[h:h + 1, :]
            score = rel if score is None else score + rel
        bits = pltpu.bitcast(score, jnp.int32)
        key = bits ^ ((bits >> 31) & 0x7FFFFFFF)
        key = jnp.where(key == -1, 0, key)
        if r >= q0:
            s_blk = r + lax.broadcasted_iota(jnp.int32, (KB, tq), 0)
            key = jnp.where((s_blk >> 6) <= (t_pos >> 6), key, KEY_NEG_INF)
            fix_ref[r - q0:r - q0 + KB, :] = jnp.maximum(s_blk - t_pos, 0).astype(F32)
        hi_ref[r:r + KB, :] = (key >> 16).astype(jnp.int16)
        lo_ref[r:r + KB, :] = (key ^ 0x8000).astype(jnp.int16)

    kf = float(top_k)
    t_hi, c_ge_hi = _kth_largest16_t(hi_ref, kf, n_keys, float(n_keys))
    t_hi16 = t_hi.astype(jnp.int16)
    need_lo = kf - _sum16_t(n_keys, lambda r: _flag16(hi_ref[r:r + KB, :] > t_hi16))
    for i in range(n_blk):
        r = i * KB
        lo_ref[r:r + KB, :] = jnp.where(hi_ref[r:r + KB, :] == t_hi16, lo_ref[r:r + KB, :], jnp.int16(INT16_MIN))
    n_cand = c_ge_hi - (kf - need_lo)
    t_lo, c_ge_lo = _kth_largest16_t(lo_ref, need_lo, n_keys, n_cand)
    t_lo16 = t_lo.astype(jnp.int16)

    def key_idx16(r):
        return (r + lax.broadcasted_iota(jnp.int32, (KB, tq), 0)).astype(jnp.int16)

    p_ref[...] = jnp.full((1, tq), 32767, jnp.int32)

    @pl.when(jnp.max(c_ge_lo - need_lo) > 0.0)
    def _():
        def eq_flags(r):
            at_lo = _flag16(lo_ref[r:r + KB, :] == t_lo16)
            return jnp.where(hi_ref[r:r + KB, :] == t_hi16, at_lo, jnp.int16(0))

        need_eq = need_lo - _sum16_t(n_keys, lambda r: _flag16(lo_ref[r:r + KB, :] > t_lo16))
        p = jnp.zeros((1, tq), jnp.int32)
        bit = 1 << ((n_keys - 1).bit_length() - 1)
        while bit:
            cand = p + bit
            cand16 = cand.astype(jnp.int16)
            c = _sum16_t(n_keys, lambda r, cand16=cand16: jnp.where(key_idx16(r) < cand16, eq_flags(r), jnp.int16(0)))
            p = jnp.where(c < need_eq, cand, p)
            bit >>= 1
        p_ref[...] = p

    p16 = p_ref[...].astype(jnp.int16)
    zero16 = jnp.asarray(0.0, BF16)
    ninf16 = jnp.asarray(ninf, BF16)
    for i in range(n_blk):
        r = i * KB
        hi = hi_ref[r:r + KB, :]
        lo = lo_ref[r:r + KB, :]
        idx = key_idx16(r)
        at_thr = jnp.where(lo == t_lo16, jnp.where(idx <= p16, zero16, ninf16), ninf16)
        in_class = jnp.where(lo > t_lo16, zero16, at_thr)
        mb = jnp.where(hi > t_hi16, zero16, jnp.where(hi == t_hi16, in_class, ninf16))
        if r >= q0:
            s_chunk = ((r + lax.broadcasted_iota(jnp.int32, (KB, tq), 0)) >> 6).astype(jnp.int16)
            t_chunk = ((q0 + lax.broadcasted_iota(jnp.int32, (KB, tq), 1)) >> 6).astype(jnp.int16)
            mb = jnp.where(s_chunk <= t_chunk, mb, ninf16)
        mb_ref[r:r + KB, :] = mb

    for h in range(N_HEADS_A):
        fix_scale = -2.0 * LOG2E * 2.0 ** (-8.0 * (h + 1) / N_HEADS_A)
        qx = qx_ref[h]

        def logits(r):
            s = lax.dot_general(kx_ref[r:r + KB, :], qx, NT_DIMS, preferred_element_type=F32)
            if r >= q0:
                s = s + fix_scale * fix_ref[r - q0:r - q0 + KB, :]
            return s

        accs = [None] * N_ACC
        for i in range(n_blk):
            r = i * KB
            m = _fold16(logits(r).astype(BF16) + mb_ref[r:r + KB, :], jnp.maximum)
            j = i % N_ACC
            accs[j] = m if accs[j] is None else jnp.maximum(accs[j], m)
        accs = [a for a in accs if a is not None]
        mx = accs[0]
        for a in accs[1:]:
            mx = jnp.maximum(mx, a)
        mx = jnp.max(mx.astype(F32), axis=0, keepdims=True)
        for i in range(n_blk):
            r = i * KB
            pt_ref[r:r + KB, :] = jnp.exp2((logits(r) - mx).astype(BF16) + mb_ref[r:r + KB, :])
        ot = jnp.dot(vt_ref[...], pt_ref[...], preferred_element_type=F32)
        ot_ref[h * DH_A:(h + 1) * DH_A, :] = ot[:DH_A, :] / ot[DH_A:DH_A + 1, :]
    o_ref[...] = ot_ref[...].T.astype(BF16)


def _dsa_t(aq, iq, misc, acoef, k, v, ik, *, batch, q_len, tq, q0, top_k):
    n_keys = q0 + tq
    per_b = q_len // tq
    blk0 = q0 // tq

    def qmap(b):
        return (b * per_b + blk0, 0)

    kvspec = pl.BlockSpec((1, n_keys, DH_A), lambda b: (b, 0, 0))
    return pl.pallas_call(
        functools.partial(_dsa_t_kernel, tq=tq, n_keys=n_keys, top_k=top_k),
        grid=(batch,),
        in_specs=[
            pl.BlockSpec((tq, N_HEADS_A * DH_A), qmap),
            pl.BlockSpec((tq, N_IDX * D_IDX), qmap),
            pl.BlockSpec((tq, 128), qmap),
            pl.BlockSpec((N_HEADS_A, 128), lambda b: (0, 0)),
            kvspec, kvspec, kvspec,
        ],
        out_specs=pl.BlockSpec((tq, N_HEADS_A * DH_A), lambda b: (b, 0)),
        out_shape=jax.ShapeDtypeStruct((batch * tq, N_HEADS_A * DH_A), BF16),
        scratch_shapes=[
            pltpu.VMEM((n_keys, 128), BF16),
            pltpu.VMEM((n_keys, 128), F32),
            pltpu.VMEM((128, n_keys), BF16),
            pltpu.VMEM((N_HEADS_A, tq, 128), BF16),
            pltpu.VMEM((N_IDX, tq, D_IDX), BF16),
            pltpu.VMEM((n_keys, tq), jnp.int16),
            pltpu.VMEM((n_keys, tq), jnp.int16),
            pltpu.VMEM((n_keys, tq), BF16),
            pltpu.VMEM((tq, tq), F32),
            pltpu.VMEM((n_keys, tq), BF16),
            pltpu.VMEM((N_HEADS_A * DH_A, tq), F32),
            pltpu.VMEM((1, tq), jnp.int32),
        ],
        compiler_params=pltpu.CompilerParams(
            dimension_semantics=("parallel",), vmem_limit_bytes=V7X_VMEM_LIMIT),
        name="dsa_t",
    )(aq, iq, misc, acoef, k, v, ik)


def _alibi_coef():
    slopes = jnp.exp2(-8.0 * jnp.arange(1, N_HEADS_A + 1, dtype=F32) / N_HEADS_A) * LOG2E
    a_hi = slopes.astype(BF16)
    a_lo = (slopes - a_hi.astype(F32)).astype(BF16)
    cols = jnp.stack([a_hi.astype(F32) * CHUNK, a_hi.astype(F32), a_lo.astype(F32) * CHUNK, a_lo.astype(F32)], axis=1)
    return jnp.zeros((N_HEADS_A, 128), F32).at[:, QX_ALIBI:QX_ALIBI + 4].set(cols)


def _mlstm_kernel(mq_ref, mk_ref, mv_ref, misc_ref, mo_ref, gn_ref, c0_ref, n0_ref, m0_ref,
                  o_ref, c_out, n_out, m_out, c_ref, n_ref, m_ref, *, lc):
    ci = pl.program_id(1)

    @pl.when(ci == 0)
    def _():
        c_ref[...] = c0_ref[0]
        n_ref[...] = n0_ref[0]
        m_ref[...] = m0_ref[0]

    misc = misc_ref[...]
    lane = lax.broadcasted_iota(jnp.int32, (8, 128), 1)
    sub = lax.broadcasted_iota(jnp.int32, (8, 128), 0)
    pick = jnp.where(lane == sub + MISC_IG, 1.0, 0.0)
    g_rows = lax.dot_general(pick, misc, NT_DIMS, preferred_element_type=F32, precision=lax.Precision.HIGHEST)
    g_cols = misc[:, MISC_IG:MISC_IG + 2 * NH_M]

    def log_sigmoid(x):
        return jnp.minimum(x, 0.0) - jnp.log1p(jnp.exp(-jnp.abs(x)))

    lf_rows = log_sigmoid(g_rows[NH_M:, :])
    lf_cols = log_sigmoid(g_cols[:, NH_M:])
    ti = lax.broadcasted_iota(jnp.int32, (lc, lc), 0)
    si = lax.broadcasted_iota(jnp.int32, (lc, lc), 1)
    causal = si <= ti
    tri = jnp.where(causal, 1.0, 0.0)
    b_cols = jnp.dot(tri, lf_cols, preferred_element_type=F32, precision=lax.Precision.HIGHEST)
    b_rows = lax.dot_general(lf_rows, tri, NT_DIMS, preferred_element_type=F32,
                             precision=lax.Precision.HIGHEST)

    ninf = float("-inf")
    for h in range(NH_M):
        q = mq_ref[:, h * DK_M:(h + 1) * DK_M]
        k = mk_ref[:, h * DK_M:(h + 1) * DK_M]
        v = mv_ref[:, h * DV_M:(h + 1) * DV_M]
        b_col = b_cols[:, h:h + 1]
        b_row = b_rows[h:h + 1, :]
        ig_row = g_rows[h:h + 1, :]
        ig_col = g_cols[:, h:h + 1]
        m_prev = m_ref[:, h:h + 1]
        c_prev = c_ref[h]
        n_prev = n_ref[h:h + 1, :]

        d = jnp.where(causal, b_col - b_row + ig_row, ninf)
        g = b_col + m_prev
        m_t = jnp.maximum(g, jnp.max(d, axis=1, keepdims=True))
        dw = jnp.exp(d - m_t)
        gw = jnp.exp(g - m_t)
        s = lax.dot_general(q, k, NT_DIMS, preferred_element_type=F32) * dw
        qc = lax.dot_general(q, c_prev.astype(BF16), NT_DIMS, preferred_element_type=F32)
        num = jnp.dot(s.astype(BF16), v, preferred_element_type=F32) + gw * qc
        qf = q.astype(F32)
        nq = jnp.sum(s, axis=1, keepdims=True) + gw * jnp.sum(qf * n_prev, axis=1, keepdims=True)
        hh = num / jnp.maximum(jnp.abs(nq), jnp.exp(-m_t))
        hh = hh * lax.rsqrt(jnp.mean(hh * hh, axis=1, keepdims=True) + EPS)
        hsl = slice(h * DV_M, (h + 1) * DV_M)
        gate = gn_ref[:, hsl] * jax.nn.sigmoid(mo_ref[:, hsl].astype(F32))
        o_ref[:, hsl] = (hh * gate).astype(BF16)

        b_last = b_col[lc - 1:lc, :]
        m_last = m_t[lc - 1:lc, :]
        decay = gw[lc - 1:lc, :]
        w_col = jnp.exp(b_last - b_col + ig_col - m_last)
        vw = (v.astype(F32) * w_col).astype(BF16)
        c_ref[h] = decay * c_prev + lax.dot_general(vw, k, TN_DIMS, preferred_element_type=F32)
        n_ref[h:h + 1, :] = decay * n_prev + jnp.sum(k.astype(F32) * w_col, axis=0, keepdims=True)
        m_ref[:, h:h + 1] = m_last

    @pl.when(ci == pl.num_programs(1) - 1)
    def _():
        c_out[0] = c_ref[...]
        n_out[0] = n_ref[...]
        m_out[0] = m_ref[...]


def _mlstm(mq, mk, mv, misc, mo, gn, c0, n0, m0, *, batch, seq, lc):
    nc = seq // lc

    def tmap(b, c):
        return (b * nc + c, 0)

    def smap4(b, c):
        return (b, 0, 0, 0)

    def smap3(b, c):
        return (b, 0, 0)

    return pl.pallas_call(
        functools.partial(_mlstm_kernel, lc=lc),
        grid=(batch, nc),
        in_specs=[
            pl.BlockSpec((lc, NH_M * DK_M), tmap),
            pl.BlockSpec((lc, NH_M * DK_M), tmap),
            pl.BlockSpec((lc, NH_M * DV_M), tmap),
            pl.BlockSpec((lc, 128), tmap),
            pl.BlockSpec((lc, NH_M * DV_M), tmap),
            pl.BlockSpec((1, NH_M * DV_M), lambda b, c: (0, 0)),
            pl.BlockSpec((1, NH_M, DV_M, DK_M), smap4),
            pl.BlockSpec((1, NH_M, DK_M), smap3),
            pl.BlockSpec((1, 1, NH_M), smap3),
        ],
        out_specs=[
            pl.BlockSpec((lc, NH_M * DV_M), tmap),
            pl.BlockSpec((1, NH_M, DV_M, DK_M), smap4),
            pl.BlockSpec((1, NH_M, DK_M), smap3),
            pl.BlockSpec((1, 1, NH_M), smap3),
        ],
        out_shape=[
            jax.ShapeDtypeStruct((batch * seq, NH_M * DV_M), BF16),
            jax.ShapeDtypeStruct((batch, NH_M, DV_M, DK_M), F32),
            jax.ShapeDtypeStruct((batch, NH_M, DK_M), F32),
            jax.ShapeDtypeStruct((batch, 1, NH_M), F32),
        ],
        scratch_shapes=[
            pltpu.VMEM((NH_M, DV_M, DK_M), F32),
            pltpu.VMEM((NH_M, DK_M), F32),
            pltpu.VMEM((1, NH_M), F32),
        ],
        compiler_params=pltpu.CompilerParams(
            dimension_semantics=("parallel", "arbitrary"), vmem_limit_bytes=V7X_VMEM_LIMIT),
        name="mlstm",
    )(mq, mk, mv, misc, mo, gn, c0, n0, m0)


def _mix_kernel(x_ref, attn_ref, hm_ref, ga_ref, gb_ref, wba_ref, wbb_ref, wo_ref, o_ref):
    a = jnp.dot(attn_ref[...], wba_ref[...], preferred_element_type=F32)
    b = jnp.dot(hm_ref[...], wbb_ref[...], preferred_element_type=F32)
    y = jax.nn.sigmoid(ga_ref[...].astype(F32)) * a + jax.nn.sigmoid(gb_ref[...].astype(F32)) * b
    o_ref[...] = x_ref[...] + jnp.dot(y.astype(BF16), wo_ref[...], preferred_element_type=F32)


def _mix(x, attn, hm, ga, gb, wba, wbb, wo, *, tm):
    n = x.shape[0]

    def tok(w):
        return pl.BlockSpec((tm, w), lambda i: (i, 0))

    return pl.pallas_call(
        _mix_kernel,
        grid=(n // tm,),
        in_specs=[tok(D_MODEL), tok(512), tok(512), tok(D_MODEL), tok(D_MODEL),
                  _const_spec((512, D_MODEL)), _const_spec((512, D_MODEL)), _const_spec((D_MODEL, D_MODEL))],
        out_specs=tok(D_MODEL),
        out_shape=jax.ShapeDtypeStruct((n, D_MODEL), F32),
        compiler_params=pltpu.CompilerParams(
            dimension_semantics=("parallel",), vmem_limit_bytes=V7X_VMEM_LIMIT),
        name="mix",
    )(x, attn, hm, ga, gb, wba, wbb, wo)


DSA_TQ = 256
MLSTM_CHUNK = 256


def _regroup_w_in(w_in, b_gates):
    offs = [0]
    for s in SPLITS:
        offs.append(offs[-1] + s)
    aq, ak, av, iq, ik, iw, mq, mk, mv, mif, mo, gates = [w_in[:, offs[i]:offs[i + 1]] for i in range(len(SPLITS))]
    pad = jnp.zeros((D_MODEL, 128 - D_IDX - N_IDX - 2 * NH_M), w_in.dtype)
    w = jnp.concatenate([aq * (DH_A ** -0.5 * LOG2E), ak, av, iq, ik, iw, mif, pad, mq * DK_M ** -0.5, mk, mv, mo, gates],
                        axis=1)
    bias = jnp.zeros((1, 128), F32).at[0, MISC_IG:MISC_IG + 2 * NH_M].set(b_gates.astype(F32))
    return w.astype(BF16), bias


def _group(x, layer, *, batch, seq, tm, prompt, cache=None, state=None):
    n = batch * seq
    x = x.reshape(n, D_MODEL)
    x1 = _ffn(x, layer["g1"], *layer["ffn1"], layer["gf"], final_norm=False, tm=tm)
    aq, ak, av, iq, ik, misc, mq, mk, mv, mo, ga, gb = _proj(x1, layer["gmix"], layer["w_in"], layer["bias"], tm=tm)
    ak3, av3, ik3 = (a.reshape(batch, seq, DH_A) for a in (ak, av, ik))
    if prompt:
        top_k = min(TOPK_MAX, seq // 4)
        acoef = _alibi_coef()
        parts = [_dsa_t(aq, iq, misc, acoef, ak3, av3, ik3, batch=batch, q_len=seq, tq=DSA_TQ, q0=q0,
                        top_k=top_k).reshape(batch, DSA_TQ, -1) for q0 in range(0, seq, DSA_TQ)]
        attn = jnp.concatenate(parts, axis=1).reshape(n, -1)
        c0 = jnp.zeros((batch, NH_M, DV_M, DK_M), F32)
        n0 = jnp.zeros((batch, NH_M, DK_M), F32)
        m0 = jnp.zeros((batch, 1, NH_M), F32)
        lc = MLSTM_CHUNK
    else:
        ck, cv, cik = cache
        past = ck.shape[1]
        total = past + seq
        n_keys = -(-total // 128) * 128
        zpad = jnp.zeros((batch, n_keys - total, DH_A), F32)
        k_all = jnp.concatenate([ck, ak3, zpad], axis=1)
        v_all = jnp.concatenate([cv, av3, zpad], axis=1)
        ik_all = jnp.concatenate([cik, ik3, zpad], axis=1)
        attn = _dsa(aq, iq, misc, k_all, v_all, ik_all, batch=batch, q_len=seq, tq=seq, q_row0=0, n_q=seq,
                    n_keys=n_keys, q_base=past, n_valid=total, top_k=min(TOPK_MAX, total // 4))
        c0, n0, m0 = state
        c0 = c0.astype(F32)
        n0 = n0.astype(F32)
        m0 = m0.astype(F32).reshape(batch, 1, NH_M)
        lc = seq
    hm, c_new, n_new, m_new = _mlstm(mq, mk, mv, misc, mo, layer["gn"], c0, n0, m0, batch=batch, seq=seq, lc=lc)
    x2 = _mix(x1, attn, hm, ga, gb, layer["wba"], layer["wbb"], layer["wo"], tm=tm)
    y = _ffn(x2, layer["g2"], *layer["ffn2"], layer["gf"], final_norm=True, tm=tm)
    return (y.reshape(batch, seq, D_MODEL), ak3, av3, ik3, c_new, n_new, m_new.reshape(batch, NH_M))


def kernel(x_prompt, x_sample, cache_attn_k, cache_attn_v, cache_idx_k, state_mlstm_C, state_mlstm_n,
           state_mlstm_m, norm_ffn1, w_ffn1_up, w_ffn1_down, norm_mix, w_in, b_mlstm_gates, norm_mlstm_heads,
           w_branch_attn, w_branch_mlstm, w_out, norm_ffn2, w_ffn2_up, w_ffn2_down, norm_final):
    assert w_in.shape[0] == 1, "single-layer trunk"
    bp, tp, _ = x_prompt.shape
    bs, ts, _ = x_sample.shape

    def ffn_w(w_up, w_down):
        return (w_up[0, :, :D_FF].astype(BF16), w_up[0, :, D_FF:].astype(BF16), w_down[0].astype(BF16))

    w_r, bias = _regroup_w_in(w_in[0], b_mlstm_gates[0])
    layer = dict(
        g1=norm_ffn1[0].reshape(1, -1), ffn1=ffn_w(w_ffn1_up, w_ffn1_down),
        g2=norm_ffn2[0].reshape(1, -1), ffn2=ffn_w(w_ffn2_up, w_ffn2_down),
        gmix=norm_mix[0].reshape(1, -1), w_in=w_r, bias=bias,
        gn=norm_mlstm_heads[0].reshape(1, -1).astype(F32),
        wba=w_branch_attn[0].astype(BF16), wbb=w_branch_mlstm[0].astype(BF16), wo=w_out[0].astype(BF16),
        gf=norm_final.reshape(1, -1),
    )
    yp, kp, vp, ikp, cp, np_, mp = _group(x_prompt, layer, batch=bp, seq=tp, tm=512, prompt=True)
    ys, ks, vs, iks, cs, ns, ms = _group(
        x_sample, layer, batch=bs, seq=ts, tm=bs * ts, prompt=False,
        cache=(cache_attn_k[0], cache_attn_v[0], cache_idx_k[0]),
        state=(state_mlstm_C[0], state_mlstm_n[0], state_mlstm_m[0]))
    dk, dv, di = cache_attn_k.dtype, cache_attn_v.dtype, cache_idx_k.dtype
    dc, dn, dm = state_mlstm_C.dtype, state_mlstm_n.dtype, state_mlstm_m.dtype
    return (yp, ys,
            kp[None].astype(dk), vp[None].astype(dv), ikp[None].astype(di),
            cp[None].astype(dc), np_[None].astype(dn), mp[None].astype(dm),
            ks[None].astype(dk), vs[None].astype(dv), iks[None].astype(di),
            cs[None].astype(dc), ns[None].astype(dn), ms[None].astype(dm))
```

```python
import functools

import jax
import jax.numpy as jnp
from jax import lax
from jax.experimental import pallas as pl
from jax.experimental.pallas import tpu as pltpu

F32 = jnp.float32
BF16 = jnp.bfloat16

D_MODEL = 1024
CHUNK = 64
N_HEADS_A = 8
DH_A = 64
N_IDX = 4
D_IDX = 64
TOPK_MAX = 256
NH_M = 4
DK_M = 64
DV_M = 128
D_FF = 2816
EPS = 1e-6
IDX_SCALE = (N_IDX * D_IDX) ** -0.5
SPLITS = (N_HEADS_A * DH_A, DH_A, DH_A, N_IDX * D_IDX, D_IDX, N_IDX,
          NH_M * DK_M, NH_M * DK_M, NH_M * DV_M, 2 * NH_M, NH_M * DV_M, 2 * D_MODEL)

C_AQ = 0
C_KV = 512
C_IQ = 640
C_MISC = 896
C_MQ = 1024
C_MK = 1280
C_MV = 1536
C_MO = 2048
C_GA = 2560
C_GB = 3584
D_INR = 4608
MISC_IW = D_IDX
MISC_IG = D_IDX + N_IDX
MISC_FG = MISC_IG + NH_M

V7X_VMEM_LIMIT = 56 * 1024 * 1024
LOG2E = 1.4426950408889634
INT_MIN = -2 ** 31
INT16_MIN = -2 ** 15
KEY_NEG_INF = INT_MIN + 0x7FFFFF
NT_DIMS = (((1,), (1,)), ((), ()))
TN_DIMS = (((0,), (0,)), ((), ()))


def _rms(x, g):
    return x * lax.rsqrt(jnp.mean(x * x, axis=-1, keepdims=True) + EPS) * g


def _const_spec(shape):
    return pl.BlockSpec(shape, lambda *_: (0,) * len(shape), pipeline_mode=pl.Buffered(1))


FF_TILE = 256


def _ffn_kernel(x_ref, g_ref, wa_ref, wb_ref, wd_ref, gf_ref, o_ref, acc_ref, *, final_norm):
    x = x_ref[...]
    h = _rms(x, g_ref[...]).astype(BF16)
    for c in range(D_FF // FF_TILE):
        sl = slice(c * FF_TILE, (c + 1) * FF_TILE)
        a = jnp.dot(h, wa_ref[:, sl], preferred_element_type=F32)
        b = jnp.dot(h, wb_ref[:, sl], preferred_element_type=F32)
        u = (a * jax.nn.sigmoid(a) * b).astype(BF16)
        d = jnp.dot(u, wd_ref[sl, :], preferred_element_type=F32)
        if c == 0:
            acc_ref[...] = d
        else:
            acc_ref[...] += d
    y = x + 0.5 * acc_ref[...]
    if final_norm:
        y = _rms(y, gf_ref[...])
    o_ref[...] = y


def _ffn(x, g, wa, wb, wd, gf, *, final_norm, tm):
    n = x.shape[0]
    return pl.pallas_call(
        functools.partial(_ffn_kernel, final_norm=final_norm),
        grid=(n // tm,),
        in_specs=[
            pl.BlockSpec((tm, D_MODEL), lambda i: (i, 0)),
            _const_spec((1, D_MODEL)),
            _const_spec((D_MODEL, D_FF)),
            _const_spec((D_MODEL, D_FF)),
            _const_spec((D_FF, D_MODEL)),
            _const_spec((1, D_MODEL)),
        ],
        out_specs=pl.BlockSpec((tm, D_MODEL), lambda i: (i, 0)),
        out_shape=jax.ShapeDtypeStruct((n, D_MODEL), F32),
        scratch_shapes=[pltpu.VMEM((tm, D_MODEL), F32)],
        compiler_params=pltpu.CompilerParams(
            dimension_semantics=("parallel",), vmem_limit_bytes=V7X_VMEM_LIMIT),
        name="ffn_final" if final_norm else "ffn",
    )(x, g, wa, wb, wd, gf)


def _proj_kernel(x_ref, g_ref, w_ref, bias_ref, aq_ref, ak_ref, av_ref, iq_ref, ik_ref, misc_ref,
                 mq_ref, mk_ref, mv_ref, mo_ref, ga_ref, gb_ref):
    h = _rms(x_ref[...], g_ref[...]).astype(BF16)

    def cols(c0, width):
        return jnp.dot(h, w_ref[:, c0:c0 + width], preferred_element_type=F32)

    aq_ref[...] = cols(C_AQ, 512).astype(BF16)
    kv = cols(C_KV, 128)
    ak_ref[...] = kv[:, :DH_A]
    av_ref[...] = kv[:, DH_A:]
    iq_ref[...] = cols(C_IQ, 256).astype(BF16)
    misc = cols(C_MISC, 128) + bias_ref[...]
    misc_ref[...] = misc
    ik_ref[...] = misc[:, :D_IDX]
    mq_ref[...] = cols(C_MQ, 256).astype(BF16)
    mk_ref[...] = cols(C_MK, 256).astype(BF16)
    mv_ref[...] = cols(C_MV, 512).astype(BF16)
    mo_ref[...] = cols(C_MO, 512).astype(BF16)
    for c in range(4):
        ga_ref[:, c * 256:(c + 1) * 256] = cols(C_GA + c * 256, 256).astype(BF16)
        gb_ref[:, c * 256:(c + 1) * 256] = cols(C_GB + c * 256, 256).astype(BF16)


def _proj(x, g, w, bias, *, tm):
    n = x.shape[0]
    widths = [(512, BF16), (DH_A, F32), (DH_A, F32), (256, BF16), (D_IDX, F32), (128, F32),
              (256, BF16), (256, BF16), (512, BF16), (512, BF16), (D_MODEL, BF16), (D_MODEL, BF16)]
    return pl.pallas_call(
        _proj_kernel,
        grid=(n // tm,),
        in_specs=[
            pl.BlockSpec((tm, D_MODEL), lambda i: (i, 0)),
            _const_spec((1, D_MODEL)),
            _const_spec((D_MODEL, D_INR)),
            _const_spec((1, 128)),
        ],
        out_specs=[pl.BlockSpec((tm, w), lambda i: (i, 0)) for w, _ in widths],
        out_shape=[jax.ShapeDtypeStruct((n, w), dt) for w, dt in widths],
        compiler_params=pltpu.CompilerParams(
            dimension_semantics=("parallel",), vmem_limit_bytes=V7X_VMEM_LIMIT),
        name="proj",
    )(x, g, w, bias)


KB = 128
N_ACC = 4
QX_ALIBI = DH_A


def _fold16(x, op):
    parts = [x[r:r + 16, :] for r in range(0, x.shape[0], 16)]
    while len(parts) > 1:
        parts = [op(parts[i], parts[i + 1]) for i in range(0, len(parts), 2)]
    return parts[0]


def _reduce_blocks(n_keys, block, op):
    accs = [None] * N_ACC
    for i, r in enumerate(range(0, n_keys, KB)):
        m = block(r)
        j = i % N_ACC
        accs[j] = m if accs[j] is None else op(accs[j], m)
    accs = [a for a in accs if a is not None]
    tot = accs[0]
    for a in accs[1:]:
        tot = op(tot, a)
    return tot


def _flag16(mask):
    return jnp.where(mask, jnp.int16(1), jnp.int16(0))


def _sum16(n_keys, flags):
    tot = _reduce_blocks(n_keys, lambda r: _fold16(flags(r), jnp.add), jnp.add)
    return jnp.sum(tot.astype(jnp.int32).astype(F32), axis=0, keepdims=True)


def _kth_largest16(ref, need, n_keys, total):
    c = _sum16(n_keys, lambda r: _flag16(ref[r:r + KB, :] >= jnp.int16(0)))
    ok = c >= need
    t = jnp.where(ok, 0, INT16_MIN).astype(jnp.int32)
    cnt = jnp.where(ok, c, total)
    for b in range(14, -1, -1):
        cand = t + (1 << b)
        cand16 = cand.astype(jnp.int16)
        c = _sum16(n_keys, lambda r, cand16=cand16: _flag16(ref[r:r + KB, :] >= cand16))
        ok = c >= need
        t = jnp.where(ok, cand, t)
        cnt = jnp.where(ok, c, cnt)
    return t, cnt


def _store_keys(score, hi_ref, lo_ref, r, admissible, lanes=slice(None)):
    bits = pltpu.bitcast(score, jnp.int32)
    key = bits ^ ((bits >> 31) & 0x7FFFFFFF)
    key = jnp.where(key == -1, 0, key)
    if admissible is not None:
        key = jnp.where(admissible, key, KEY_NEG_INF)
    hi_ref[r:r + KB, lanes] = (key >> 16).astype(jnp.int16)
    lo_ref[r:r + KB, lanes] = (key ^ 0x8000).astype(jnp.int16)


def _select_mask(hi_ref, lo_ref, mb_ref, p_ref, n_keys, nl, top_k, valid16):
    kf = float(top_k)
    t_hi, c_ge_hi = _kth_largest16(hi_ref, kf, n_keys, float(n_keys))
    t_hi16 = t_hi.astype(jnp.int16)
    need_lo = kf - _sum16(n_keys, lambda r: _flag16(hi_ref[r:r + KB, :] > t_hi16))
    for r in range(0, n_keys, KB):
        lo_ref[r:r + KB, :] = jnp.where(hi_ref[r:r + KB, :] == t_hi16, lo_ref[r:r + KB, :], jnp.int16(INT16_MIN))
    n_cand = c_ge_hi - (kf - need_lo)
    t_lo, c_ge_lo = _kth_largest16(lo_ref, need_lo, n_keys, n_cand)
    t_lo16 = t_lo.astype(jnp.int16)

    def key_idx16(r):
        return (r + lax.broadcasted_iota(jnp.int32, (KB, nl), 0)).astype(jnp.int16)

    p_ref[...] = jnp.full((1, nl), 32767, jnp.int32)

    @pl.when(jnp.max(c_ge_lo - need_lo) > 0.0)
    def _():
        def eq_flags(r):
            at_lo = _flag16(lo_ref[r:r + KB, :] == t_lo16)
            return jnp.where(hi_ref[r:r + KB, :] == t_hi16, at_lo, jnp.int16(0))

        need_eq = need_lo - _sum16(n_keys, lambda r: _flag16(lo_ref[r:r + KB, :] > t_lo16))
        p = jnp.zeros((1, nl), jnp.int32)
        bit = 1 << ((n_keys - 1).bit_length() - 1)
        while bit:
            cand = p + bit
            cand16 = cand.astype(jnp.int16)
            c = _sum16(n_keys, lambda r, cand16=cand16: jnp.where(key_idx16(r) < cand16, eq_flags(r), jnp.int16(0)))
            p = jnp.where(c < need_eq, cand, p)
            bit >>= 1
        p_ref[...] = p

    p16 = p_ref[...].astype(jnp.int16)
    zero16 = jnp.asarray(0.0, BF16)
    ninf16 = jnp.asarray(float("-inf"), BF16)
    for r in range(0, n_keys, KB):
        hi = hi_ref[r:r + KB, :]
        lo = lo_ref[r:r + KB, :]
        at_thr = jnp.where(lo == t_lo16, jnp.where(key_idx16(r) <= p16, zero16, ninf16), ninf16)
        in_class = jnp.where(lo > t_lo16, zero16, at_thr)
        mb = jnp.where(hi > t_hi16, zero16, jnp.where(hi == t_hi16, in_class, ninf16))
        valid = valid16(r)
        if valid is not None:
            mb = jnp.where(valid, mb, ninf16)
        mb_ref[r:r + KB, :] = mb


def _store_logits(kx_ref, lg_ref, qx, fix, n_keys):
    for r in range(0, n_keys, KB):
        s = lax.dot_general(kx_ref[r:r + KB, :], qx, NT_DIMS, preferred_element_type=F32)
        f = fix(r)
        lg_ref[r:r + KB, :] = s if f is None else s + f


def _attend_max(lg_ref, mb_ref, n_keys, lanes=slice(None)):
    mx = _reduce_blocks(
        n_keys, lambda r: _fold16(lg_ref[r:r + KB, :].astype(BF16) + mb_ref[r:r + KB, lanes], jnp.maximum),
        jnp.maximum)
    return jnp.max(mx.astype(F32), axis=0, keepdims=True)


def _attend_exp(lg_ref, mb_ref, pt_ref, mx, n_keys, lanes=slice(None)):
    for r in range(0, n_keys, KB):
        pt_ref[r:r + KB, :] = jnp.exp2((lg_ref[r:r + KB, :] - mx).astype(BF16) + mb_ref[r:r + KB, lanes])


def _attend_pv(vt_ref, pt_ref):
    return jnp.dot(vt_ref[...], pt_ref[...], preferred_element_type=F32)


def _attend_slabs(n_slabs, maxf, expf, pvf):
    mx = {}
    for step in range(n_slabs + 2):
        if step < n_slabs:
            mx[step] = maxf(step)
        if 0 <= step - 1 < n_slabs:
            expf(step - 1, mx.pop(step - 1))
        if 0 <= step - 2 < n_slabs:
            pvf(step - 2)


def _alibi_key_columns(n_keys):
    s_pos = lax.broadcasted_iota(jnp.int32, (n_keys, DH_A), 0)
    ln = lax.broadcasted_iota(jnp.int32, (n_keys, DH_A), 1)
    s_part = jnp.where((ln & 1) == 0, s_pos >> 6, s_pos & (CHUNK - 1))
    return jnp.where(ln < 4, s_part, 0).astype(F32).astype(BF16)


def _ones_column(n_keys):
    ln = lax.broadcasted_iota(jnp.int32, (n_keys, DH_A), 1)
    return jnp.where(ln == 0, 1.0, 0.0)


def _pick_rows(x, lane0):
    lane = lax.broadcasted_iota(jnp.int32, (8, 128), 1)
    sub = lax.broadcasted_iota(jnp.int32, (8, 128), 0)
    pick = jnp.where(lane == sub + lane0, 1.0, 0.0)
    return lax.dot_general(pick, x, NT_DIMS, preferred_element_type=F32, precision=lax.Precision.HIGHEST)


def _dsa_prompt_kernel(aq_ref, iq_ref, misc_ref, acoef_ref, k_ref, v_ref, ik_ref, prev_ref, o_ref,
                       kx_ref, vx_ref, vt_ref, qx_ref, iqh_ref, hi_ref, lo_ref, mb_ref, fix_ref, lg_ref, pt_ref, ot_ref,
                       p_ref, *, tq, n_keys, top_k):
    del prev_ref
    q0 = n_keys - tq

    kx_ref[:, :DH_A] = k_ref[0].astype(BF16)
    kx_ref[:, DH_A:] = _alibi_key_columns(n_keys)
    vx_ref[:, :DH_A] = v_ref[0]
    vx_ref[:, DH_A:] = _ones_column(n_keys)
    vt_ref[...] = vx_ref[...].T.astype(BF16)
    for h in range(N_HEADS_A):
        qx_ref[h, :, :DH_A] = aq_ref[:, h * DH_A:(h + 1) * DH_A]
        qx_ref[h, :, DH_A:] = jnp.broadcast_to(acoef_ref[h:h + 1, DH_A:], (tq, DH_A)).astype(BF16)
    for h in range(N_IDX):
        iqh_ref[h] = iq_ref[:, h * D_IDX:(h + 1) * D_IDX]

    iw_rows = _pick_rows(misc_ref[...], MISC_IW) * IDX_SCALE
    t_pos = q0 + lax.broadcasted_iota(jnp.int32, (KB, tq), 1)

    def key_pos(r):
        return r + lax.broadcasted_iota(jnp.int32, (KB, tq), 0)

    for r in range(0, n_keys, KB):
        ikb = ik_ref[0, r:r + KB, :].astype(BF16)
        score = None
        for h in range(N_IDX):
            rel = lax.dot_general(ikb, iqh_ref[h], NT_DIMS, preferred_element_type=F32)
            rel = jnp.maximum(rel, 0.0) * iw_rows[h:h + 1, :]
            score = rel if score is None else score + rel
        adm = None
        if r >= q0:
            adm = (key_pos(r) >> 6) <= (t_pos >> 6)
            fix_ref[r - q0:r - q0 + KB, :] = jnp.maximum(key_pos(r) - t_pos, 0).astype(F32)
        _store_keys(score, hi_ref, lo_ref, r, adm)

    for h in range(N_HEADS_A):
        fix_scale = -2.0 * LOG2E * 2.0 ** (-8.0 * (h + 1) / N_HEADS_A)
        _store_logits(kx_ref, lg_ref.at[h], qx_ref[h],
                      lambda r, c=fix_scale: c * fix_ref[r - q0:r - q0 + KB, :] if r >= q0 else None, n_keys)

    def valid16(r):
        if r < q0:
            return None
        return (key_pos(r) >> 6).astype(jnp.int16) <= (t_pos >> 6).astype(jnp.int16)

    _select_mask(hi_ref, lo_ref, mb_ref, p_ref, n_keys, tq, top_k, valid16)

    def pv(h):
        ot = _attend_pv(vt_ref, pt_ref.at[h])
        ot_ref[h * DH_A:(h + 1) * DH_A, :] = ot[:DH_A, :] / ot[DH_A:DH_A + 1, :]

    _attend_slabs(
        N_HEADS_A,
        lambda h: _attend_max(lg_ref.at[h], mb_ref, n_keys),
        lambda h, mx: _attend_exp(lg_ref.at[h], mb_ref, pt_ref.at[h], mx, n_keys),
        pv)
    o_ref[...] = ot_ref[...].T.astype(BF16)


def _dsa_prompt(aq, iq, misc, acoef, k, v, ik, prev, *, batch, q_len, tq, q0, top_k):
    n_keys = q0 + tq
    per_b = q_len // tq
    blk0 = q0 // tq

    def qmap(b):
        return (b * per_b + blk0, 0)

    kvspec = pl.BlockSpec((1, n_keys, DH_A), lambda b: (b, 0, 0))
    return pl.pallas_call(
        functools.partial(_dsa_prompt_kernel, tq=tq, n_keys=n_keys, top_k=top_k),
        grid=(batch,),
        in_specs=[
            pl.BlockSpec((tq, N_HEADS_A * DH_A), qmap),
            pl.BlockSpec((tq, N_IDX * D_IDX), qmap),
            pl.BlockSpec((tq, 128), qmap),
            pl.BlockSpec((N_HEADS_A, 128), lambda b: (0, 0)),
            kvspec, kvspec, kvspec,
            pl.BlockSpec(memory_space=pl.ANY),
        ],
        out_specs=pl.BlockSpec((tq, N_HEADS_A * DH_A), qmap),
        out_shape=jax.ShapeDtypeStruct((batch * q_len, N_HEADS_A * DH_A), BF16),
        input_output_aliases={7: 0},
        scratch_shapes=[
            pltpu.VMEM((n_keys, 128), BF16),
            pltpu.VMEM((n_keys, 128), F32),
            pltpu.VMEM((128, n_keys), BF16),
            pltpu.VMEM((N_HEADS_A, tq, 128), BF16),
            pltpu.VMEM((N_IDX, tq, D_IDX), BF16),
            pltpu.VMEM((n_keys, tq), jnp.int16),
            pltpu.VMEM((n_keys, tq), jnp.int16),
            pltpu.VMEM((n_keys, tq), BF16),
            pltpu.VMEM((tq, tq), F32),
            pltpu.VMEM((N_HEADS_A, n_keys, tq), F32),
            pltpu.VMEM((N_HEADS_A, n_keys, tq), BF16),
            pltpu.VMEM((N_HEADS_A * DH_A, tq), F32),
            pltpu.VMEM((1, tq), jnp.int32),
        ],
        compiler_params=pltpu.CompilerParams(
            dimension_semantics=("parallel",), vmem_limit_bytes=V7X_VMEM_LIMIT),
        name="dsa_prompt",
    )(aq, iq, misc, acoef, k, v, ik, prev)


def _dsa_sample_kernel(aq_ref, iq_ref, misc_ref, acoef_ref, ck_ref, cv_ref, cik_ref, nk_ref, nv_ref, nik_ref, o_ref,
                       kx_ref, ikx_ref, vx_ref, vt_ref, qx_ref, iqx_ref, hi_ref, lo_ref, mb_ref, lg_ref, pt_ref, p_ref,
                       *, group, t, past, n_keys, top_k):
    total = past + t
    nl = N_HEADS_A * t
    last = n_keys - KB
    assert past % KB == 0 and total <= n_keys and past >= last and (past + t - 1) // CHUNK == past // CHUNK
    lane = lax.broadcasted_iota(jnp.int32, (KB, nl), 1)
    row = last + lax.broadcasted_iota(jnp.int32, (KB, nl), 0)
    real = row < total
    slope = jnp.exp2(-8.0 * ((lane // t) + 1).astype(F32) / N_HEADS_A)
    fix_last = (-2.0 * LOG2E) * slope * jnp.maximum(row - (past + lane % t), 0).astype(F32)
    sub = lax.broadcasted_iota(jnp.int32, (t, nl), 0)
    same_q = (lax.broadcasted_iota(jnp.int32, (t, nl), 1) % t) == sub

    for g in range(group):
        rows = slice(g * t, (g + 1) * t)
        lanes = slice(g * nl, (g + 1) * nl)
        kx, ikx, vx = kx_ref.at[g], ikx_ref.at[g], vx_ref.at[g]
        for dst, cache, new in ((kx, ck_ref, nk_ref), (ikx, cik_ref, nik_ref), (vx, cv_ref, nv_ref)):
            dst[:past, :DH_A] = cache[g].astype(dst.dtype)
            dst[past:total, :DH_A] = new[g].astype(dst.dtype)
            dst[total:, :DH_A] = jnp.zeros((n_keys - total, DH_A), dst.dtype)
        kx[:, DH_A:] = _alibi_key_columns(n_keys)
        vx[:, DH_A:] = _ones_column(n_keys)
        vt_ref[g] = vx[...].T.astype(BF16)
        for h in range(N_HEADS_A):
            qx_ref[g, h * t:(h + 1) * t, :DH_A] = aq_ref[rows, h * DH_A:(h + 1) * DH_A]
            qx_ref[g, h * t:(h + 1) * t, DH_A:] = jnp.broadcast_to(acoef_ref[h:h + 1, DH_A:], (t, DH_A)).astype(BF16)
        for h in range(N_IDX):
            iqx_ref[g, h * t:(h + 1) * t, :] = iq_ref[rows, h * D_IDX:(h + 1) * D_IDX]
        iqx_ref[g, N_IDX * t:, :] = jnp.zeros((nl - N_IDX * t, D_IDX), BF16)

        misc = misc_ref[rows, :]
        wmat = jnp.concatenate(
            [jnp.where(same_q, misc[:, MISC_IW + h:MISC_IW + h + 1] * IDX_SCALE, 0.0) for h in range(N_IDX)]
            + [jnp.zeros((nl - N_IDX * t, nl), F32)], axis=0)
        for r in range(0, n_keys, KB):
            rel = lax.dot_general(ikx[r:r + KB, :], iqx_ref[g], NT_DIMS, preferred_element_type=F32)
            score = jnp.dot(jnp.maximum(rel, 0.0), wmat, preferred_element_type=F32,
                            precision=lax.Precision.HIGHEST)
            _store_keys(score, hi_ref, lo_ref, r, real if r == last else None, lanes)
        _store_logits(kx, lg_ref.at[g], qx_ref[g], lambda r: fix_last if r == last else None, n_keys)

    real16 = (last + lax.broadcasted_iota(jnp.int32, (KB, group * nl), 0)).astype(jnp.int16) < jnp.int16(total)
    _select_mask(hi_ref, lo_ref, mb_ref, p_ref, n_keys, group * nl, top_k, lambda r: real16 if r == last else None)

    def lanes_of(g):
        return slice(g * nl, (g + 1) * nl)

    def pv(g):
        res = _attend_pv(vt_ref.at[g], pt_ref.at[g]).T
        res = res[:, :DH_A] / res[:, DH_A:DH_A + 1]
        for h in range(N_HEADS_A):
            o_ref[g * t:(g + 1) * t, h * DH_A:(h + 1) * DH_A] = res[h * t:(h + 1) * t, :].astype(BF16)

    _attend_slabs(
        group,
        lambda g: _attend_max(lg_ref.at[g], mb_ref, n_keys, lanes_of(g)),
        lambda g, mx: _attend_exp(lg_ref.at[g], mb_ref, pt_ref.at[g], mx, n_keys, lanes_of(g)),
        pv)


def _dsa_sample(aq, iq, misc, acoef, ck, cv, cik, nk, nv, nik, *, batch, t, group, top_k):
    past = ck.shape[1]
    n_keys = -(-(past + t) // KB) * KB
    nl = N_HEADS_A * t
    assert nl == 128 and batch % group == 0

    def tok(w):
        return pl.BlockSpec((group * t, w), lambda i: (i, 0))

    cspec = pl.BlockSpec((group, past, DH_A), lambda i: (i, 0, 0))
    nspec = pl.BlockSpec((group, t, DH_A), lambda i: (i, 0, 0))
    return pl.pallas_call(
        functools.partial(_dsa_sample_kernel, group=group, t=t, past=past, n_keys=n_keys, top_k=top_k),
        grid=(batch // group,),
        in_specs=[tok(N_HEADS_A * DH_A), tok(N_IDX * D_IDX), tok(128),
                  pl.BlockSpec((N_HEADS_A, 128), lambda i: (0, 0)),
                  cspec, cspec, cspec, nspec, nspec, nspec],
        out_specs=tok(N_HEADS_A * DH_A),
        out_shape=jax.ShapeDtypeStruct((batch * t, N_HEADS_A * DH_A), BF16),
        scratch_shapes=[
            pltpu.VMEM((group, n_keys, 128), BF16),
            pltpu.VMEM((group, n_keys, D_IDX), BF16),
            pltpu.VMEM((group, n_keys, 128), F32),
            pltpu.VMEM((group, 128, n_keys), BF16),
            pltpu.VMEM((group, nl, 128), BF16),
            pltpu.VMEM((group, nl, D_IDX), BF16),
            pltpu.VMEM((n_keys, group * nl), jnp.int16),
            pltpu.VMEM((n_keys, group * nl), jnp.int16),
            pltpu.VMEM((n_keys, group * nl), BF16),
            pltpu.VMEM((group, n_keys, nl), F32),
            pltpu.VMEM((group, n_keys, nl), BF16),
            pltpu.VMEM((1, group * nl), jnp.int32),
        ],
        compiler_params=pltpu.CompilerParams(
            dimension_semantics=("parallel",), vmem_limit_bytes=V7X_VMEM_LIMIT),
        name="dsa_sample",
    )(aq, iq, misc, acoef, ck, cv, cik, nk, nv, nik)


def _alibi_coef():
    slopes = jnp.exp2(-8.0 * jnp.arange(1, N_HEADS_A + 1, dtype=F32) / N_HEADS_A) * LOG2E
    a_hi = slopes.astype(BF16)
    a_lo = (slopes - a_hi.astype(F32)).astype(BF16)
    cols = jnp.stack([a_hi.astype(F32) * CHUNK, a_hi.astype(F32), a_lo.astype(F32) * CHUNK, a_lo.astype(F32)], axis=1)
    return jnp.zeros((N_HEADS_A, 128), F32).at[:, QX_ALIBI:QX_ALIBI + 4].set(cols)


def _mlstm_kernel(mq_ref, mk_ref, mv_ref, misc_ref, mo_ref, gn_ref, c0_ref, n0_ref, m0_ref,
                  o_ref, c_out, n_out, m_out, c_ref, n_ref, m_ref, *, lc):
    ci = pl.program_id(1)

    @pl.when(ci == 0)
    def _():
        c_ref[...] = c0_ref[0]
        n_ref[...] = n0_ref[0]
        m_ref[...] = m0_ref[0]

    misc = misc_ref[...]
    lane = lax.broadcasted_iota(jnp.int32, (8, 128), 1)
    sub = lax.broadcasted_iota(jnp.int32, (8, 128), 0)
    pick = jnp.where(lane == sub + MISC_IG, 1.0, 0.0)
    g_rows = lax.dot_general(pick, misc, NT_DIMS, preferred_element_type=F32, precision=lax.Precision.HIGHEST)
    g_cols = misc[:, MISC_IG:MISC_IG + 2 * NH_M]

    def log_sigmoid(x):
        return jnp.minimum(x, 0.0) - jnp.log1p(jnp.exp(-jnp.abs(x)))

    lf_rows = log_sigmoid(g_rows[NH_M:, :])
    lf_cols = log_sigmoid(g_cols[:, NH_M:])
    ti = lax.broadcasted_iota(jnp.int32, (lc, lc), 0)
    si = lax.broadcasted_iota(jnp.int32, (lc, lc), 1)
    causal = si <= ti
    tri = jnp.where(causal, 1.0, 0.0)
    b_cols = jnp.dot(tri, lf_cols, preferred_element_type=F32, precision=lax.Precision.HIGHEST)
    b_rows = lax.dot_general(lf_rows, tri, NT_DIMS, preferred_element_type=F32,
                             precision=lax.Precision.HIGHEST)

    ninf = float("-inf")
    for h in range(NH_M):
        q = mq_ref[:, h * DK_M:(h + 1) * DK_M]
        k = mk_ref[:, h * DK_M:(h + 1) * DK_M]
        v = mv_ref[:, h * DV_M:(h + 1) * DV_M]
        b_col = b_cols[:, h:h + 1]
        b_row = b_rows[h:h + 1, :]
        ig_row = g_rows[h:h + 1, :]
        ig_col = g_cols[:, h:h + 1]
        m_prev = m_ref[:, h:h + 1]
        c_prev = c_ref[h]
        n_prev = n_ref[h:h + 1, :]

        d = jnp.where(causal, b_col - b_row + ig_row, ninf)
        g = b_col + m_prev
        m_t = jnp.maximum(g, jnp.max(d, axis=1, keepdims=True))
        dw = jnp.exp(d - m_t)
        gw = jnp.exp(g - m_t)
        s = lax.dot_general(q, k, NT_DIMS, preferred_element_type=F32) * dw
        qc = lax.dot_general(q, c_prev.astype(BF16), NT_DIMS, preferred_element_type=F32)
        num = jnp.dot(s.astype(BF16), v, preferred_element_type=F32) + gw * qc
        qf = q.astype(F32)
        nq = jnp.sum(s, axis=1, keepdims=True) + gw * jnp.sum(qf * n_prev, axis=1, keepdims=True)
        hh = num / jnp.maximum(jnp.abs(nq), jnp.exp(-m_t))
        hh = hh * lax.rsqrt(jnp.mean(hh * hh, axis=1, keepdims=True) + EPS)
        hsl = slice(h * DV_M, (h + 1) * DV_M)
        gate = gn_ref[:, hsl] * jax.nn.sigmoid(mo_ref[:, hsl].astype(F32))
        o_ref[:, hsl] = (hh * gate).astype(BF16)

        b_last = b_col[lc - 1:lc, :]
        m_last = m_t[lc - 1:lc, :]
        decay = gw[lc - 1:lc, :]
        w_col = jnp.exp(b_last - b_col + ig_col - m_last)
        vw = (v.astype(F32) * w_col).astype(BF16)
        c_ref[h] = decay * c_prev + lax.dot_general(vw, k, TN_DIMS, preferred_element_type=F32)
        n_ref[h:h + 1, :] = decay * n_prev + jnp.sum(k.astype(F32) * w_col, axis=0, keepdims=True)
        m_ref[:, h:h + 1] = m_last

    @pl.when(ci == pl.num_programs(1) - 1)
    def _():
        c_out[0] = c_ref[...]
        n_out[0] = n_ref[...]
        m_out[0] = m_ref[...]


def _mlstm(mq, mk, mv, misc, mo, gn, c0, n0, m0, *, batch, seq, lc):
    nc = seq // lc

    def tmap(b, c):
        return (b * nc + c, 0)

    def smap4(b, c):
        return (b, 0, 0, 0)

    def smap3(b, c):
        return (b, 0, 0)

    return pl.pallas_call(
        functools.partial(_mlstm_kernel, lc=lc),
        grid=(batch, nc),
        in_specs=[
            pl.BlockSpec((lc, NH_M * DK_M), tmap),
            pl.BlockSpec((lc, NH_M * DK_M), tmap),
            pl.BlockSpec((lc, NH_M * DV_M), tmap),
            pl.BlockSpec((lc, 128), tmap),
            pl.BlockSpec((lc, NH_M * DV_M), tmap),
            pl.BlockSpec((1, NH_M * DV_M), lambda b, c: (0, 0)),
            pl.BlockSpec((1, NH_M, DV_M, DK_M), smap4),
            pl.BlockSpec((1, NH_M, DK_M), smap3),
            pl.BlockSpec((1, 1, NH_M), smap3),
        ],
        out_specs=[
            pl.BlockSpec((lc, NH_M * DV_M), tmap),
            pl.BlockSpec((1, NH_M, DV_M, DK_M), smap4),
            pl.BlockSpec((1, NH_M, DK_M), smap3),
            pl.BlockSpec((1, 1, NH_M), smap3),
        ],
        out_shape=[
            jax.ShapeDtypeStruct((batch * seq, NH_M * DV_M), BF16),
            jax.ShapeDtypeStruct((batch, NH_M, DV_M, DK_M), F32),
            jax.ShapeDtypeStruct((batch, NH_M, DK_M), F32),
            jax.ShapeDtypeStruct((batch, 1, NH_M), F32),
        ],
        scratch_shapes=[
            pltpu.VMEM((NH_M, DV_M, DK_M), F32),
            pltpu.VMEM((NH_M, DK_M), F32),
            pltpu.VMEM((1, NH_M), F32),
        ],
        compiler_params=pltpu.CompilerParams(
            dimension_semantics=("parallel", "arbitrary"), vmem_limit_bytes=V7X_VMEM_LIMIT),
        name="mlstm",
    )(mq, mk, mv, misc, mo, gn, c0, n0, m0)


def _mix_kernel(x_ref, attn_ref, hm_ref, ga_ref, gb_ref, wba_ref, wbb_ref, wo_ref, o_ref):
    a = jnp.dot(attn_ref[...], wba_ref[...], preferred_element_type=F32)
    b = jnp.dot(hm_ref[...], wbb_ref[...], preferred_element_type=F32)
    y = jax.nn.sigmoid(ga_ref[...].astype(F32)) * a + jax.nn.sigmoid(gb_ref[...].astype(F32)) * b
    o_ref[...] = x_ref[...] + jnp.dot(y.astype(BF16), wo_ref[...], preferred_element_type=F32)


def _mix(x, attn, hm, ga, gb, wba, wbb, wo, *, tm):
    n = x.shape[0]

    def tok(w):
        return pl.BlockSpec((tm, w), lambda i: (i, 0))

    return pl.pallas_call(
        _mix_kernel,
        grid=(n // tm,),
        in_specs=[tok(D_MODEL), tok(512), tok(512), tok(D_MODEL), tok(D_MODEL),
                  _const_spec((512, D_MODEL)), _const_spec((512, D_MODEL)), _const_spec((D_MODEL, D_MODEL))],
        out_specs=tok(D_MODEL),
        out_shape=jax.ShapeDtypeStruct((n, D_MODEL), F32),
        compiler_params=pltpu.CompilerParams(
            dimension_semantics=("parallel",), vmem_limit_bytes=V7X_VMEM_LIMIT),
        name="mix",
    )(x, attn, hm, ga, gb, wba, wbb, wo)


DSA_TQ = 256
DSA_SAMPLE_GROUP = 4
MLSTM_CHUNK = 256


def _regroup_w_in(w_in, b_gates):
    offs = [0]
    for s in SPLITS:
        offs.append(offs[-1] + s)
    aq, ak, av, iq, ik, iw, mq, mk, mv, mif, mo, gates = [w_in[:, offs[i]:offs[i + 1]] for i in range(len(SPLITS))]
    pad = jnp.zeros((D_MODEL, 128 - D_IDX - N_IDX - 2 * NH_M), w_in.dtype)
    w = jnp.concatenate([aq * (DH_A ** -0.5 * LOG2E), ak, av, iq, ik, iw, mif, pad, mq * DK_M ** -0.5, mk, mv, mo, gates],
                        axis=1)
    bias = jnp.zeros((1, 128), F32).at[0, MISC_IG:MISC_IG + 2 * NH_M].set(b_gates.astype(F32))
    return w.astype(BF16), bias


def _group(x, layer, *, batch, seq, tm, prompt, cache=None, state=None):
    n = batch * seq
    x = x.reshape(n, D_MODEL)
    x1 = _ffn(x, layer["g1"], *layer["ffn1"], layer["gf"], final_norm=False, tm=tm)
    aq, ak, av, iq, ik, misc, mq, mk, mv, mo, ga, gb = _proj(x1, layer["gmix"], layer["w_in"], layer["bias"], tm=tm)
    ak3, av3, ik3 = (a.reshape(batch, seq, DH_A) for a in (ak, av, ik))
    acoef = _alibi_coef()
    if prompt:
        top_k = min(TOPK_MAX, seq // 4)
        attn = jnp.zeros((n, N_HEADS_A * DH_A), BF16)
        for q0 in range(0, seq, DSA_TQ):
            attn = _dsa_prompt(aq, iq, misc, acoef, ak3, av3, ik3, attn, batch=batch, q_len=seq, tq=DSA_TQ, q0=q0,
                               top_k=top_k)
        c0 = jnp.zeros((batch, NH_M, DV_M, DK_M), F32)
        n0 = jnp.zeros((batch, NH_M, DK_M), F32)
        m0 = jnp.zeros((batch, 1, NH_M), F32)
        lc = MLSTM_CHUNK
    else:
        ck, cv, cik = cache
        attn = _dsa_sample(aq, iq, misc, acoef, ck, cv, cik, ak3, av3, ik3, batch=batch, t=seq, group=DSA_SAMPLE_GROUP,
                           top_k=min(TOPK_MAX, (ck.shape[1] + seq) // 4))
        c0, n0, m0 = state
        c0 = c0.astype(F32)
        n0 = n0.astype(F32)
        m0 = m0.astype(F32).reshape(batch, 1, NH_M)
        lc = seq
    hm, c_new, n_new, m_new = _mlstm(mq, mk, mv, misc, mo, layer["gn"], c0, n0, m0, batch=batch, seq=seq, lc=lc)
    x2 = _mix(x1, attn, hm, ga, gb, layer["wba"], layer["wbb"], layer["wo"], tm=tm)
    y = _ffn(x2, layer["g2"], *layer["ffn2"], layer["gf"], final_norm=True, tm=tm)
    return (y.reshape(batch, seq, D_MODEL), ak3, av3, ik3, c_new, n_new, m_new.reshape(batch, NH_M))


def kernel(x_prompt, x_sample, cache_attn_k, cache_attn_v, cache_idx_k, state_mlstm_C, state_mlstm_n,
           state_mlstm_m, norm_ffn1, w_ffn1_up, w_ffn1_down, norm_mix, w_in, b_mlstm_gates, norm_mlstm_heads,
           w_branch_attn, w_branch_mlstm, w_out, norm_ffn2, w_ffn2_up, w_ffn2_down, norm_final):
    assert w_in.shape[0] == 1, "single-layer trunk"
    bp, tp, _ = x_prompt.shape
    bs, ts, _ = x_sample.shape

    def ffn_w(w_up, w_down):
        return (w_up[0, :, :D_FF].astype(BF16), w_up[0, :, D_FF:].astype(BF16), w_down[0].astype(BF16))

    w_r, bias = _regroup_w_in(w_in[0], b_mlstm_gates[0])
    layer = dict(
        g1=norm_ffn1[0].reshape(1, -1), ffn1=ffn_w(w_ffn1_up, w_ffn1_down),
        g2=norm_ffn2[0].reshape(1, -1), ffn2=ffn_w(w_ffn2_up, w_ffn2_down),
        gmix=norm_mix[0].reshape(1, -1), w_in=w_r, bias=bias,
        gn=norm_mlstm_heads[0].reshape(1, -1).astype(F32),
        wba=w_branch_attn[0].astype(BF16), wbb=w_branch_mlstm[0].astype(BF16), wo=w_out[0].astype(BF16),
        gf=norm_final.reshape(1, -1),
    )
    yp, kp, vp, ikp, cp, np_, mp = _group(x_prompt, layer, batch=bp, seq=tp, tm=512, prompt=True)
    ys, ks, vs, iks, cs, ns, ms = _group(
        x_sample, layer, batch=bs, seq=ts, tm=bs * ts, prompt=False,
        cache=(cache_attn_k[0], cache_attn_v[0], cache_idx_k[0]),
        state=(state_mlstm_C[0], state_mlstm_n[0], state_mlstm_m[0]))
    dk, dv, di = cache_attn_k.dtype, cache_attn_v.dtype, cache_idx_k.dtype
    dc, dn, dm = state_mlstm_C.dtype, state_mlstm_n.dtype, state_mlstm_m.dtype
    return (yp, ys,
            kp[None].astype(dk), vp[None].astype(dv), ikp[None].astype(di),
            cp[None].astype(dc), np_[None].astype(dn), mp[None].astype(dm),
            ks[None].astype(dk), vs[None].astype(dv), iks[None].astype(di),
            cs[None].astype(dc), ns[None].astype(dn), ms[None].astype(dm))
```

```python
import functools

import jax
import jax.numpy as jnp
from jax import lax
from jax.experimental import pallas as pl
from jax.experimental.pallas import tpu as pltpu

F32 = jnp.float32
BF16 = jnp.bfloat16

D_MODEL = 1024
CHUNK = 64
N_HEADS_A = 8
DH_A = 64
N_IDX = 4
D_IDX = 64
TOPK_MAX = 256
NH_M = 4
DK_M = 64
DV_M = 128
D_FF = 2816
EPS = 1e-6
IDX_SCALE = (N_IDX * D_IDX) ** -0.5
SPLITS = (N_HEADS_A * DH_A, DH_A, DH_A, N_IDX * D_IDX, D_IDX, N_IDX,
          NH_M * DK_M, NH_M * DK_M, NH_M * DV_M, 2 * NH_M, NH_M * DV_M, 2 * D_MODEL)

C_AQ = 0
C_KV = 512
C_IQ = 640
C_MISC = 896
C_MQ = 1024
C_MK = 1280
C_MV = 1536
C_MO = 2048
C_GA = 2560
C_GB = 3584
D_INR = 4608
MISC_IW = D_IDX
MISC_IG = D_IDX + N_IDX
MISC_FG = MISC_IG + NH_M

V7X_VMEM_LIMIT = 56 * 1024 * 1024
LOG2E = 1.4426950408889634
INT_MIN = -2 ** 31
INT16_MIN = -2 ** 15
KEY_NEG_INF = INT_MIN + 0x7FFFFF
NT_DIMS = (((1,), (1,)), ((), ()))
TN_DIMS = (((0,), (0,)), ((), ()))


def _rms(x, g):
    return x * lax.rsqrt(jnp.mean(x * x, axis=-1, keepdims=True) + EPS) * g


def _const_spec(shape):
    return pl.BlockSpec(shape, lambda *_: (0,) * len(shape), pipeline_mode=pl.Buffered(1))


FF_TILE = 256


def _ffn_kernel(x_ref, g_ref, wa_ref, wb_ref, wd_ref, gf_ref, o_ref, acc_ref, *, final_norm):
    x = x_ref[...]
    h = _rms(x, g_ref[...]).astype(BF16)
    for c in range(D_FF // FF_TILE):
        sl = slice(c * FF_TILE, (c + 1) * FF_TILE)
        a = jnp.dot(h, wa_ref[:, sl], preferred_element_type=F32)
        b = jnp.dot(h, wb_ref[:, sl], preferred_element_type=F32)
        u = (a * jax.nn.sigmoid(a) * b).astype(BF16)
        d = jnp.dot(u, wd_ref[sl, :], preferred_element_type=F32)
        if c == 0:
            acc_ref[...] = d
        else:
            acc_ref[...] += d
    y = x + 0.5 * acc_ref[...]
    if final_norm:
        y = _rms(y, gf_ref[...])
    o_ref[...] = y


def _ffn(x, g, wa, wb, wd, gf, *, final_norm, tm):
    n = x.shape[0]
    return pl.pallas_call(
        functools.partial(_ffn_kernel, final_norm=final_norm),
        grid=(n // tm,),
        in_specs=[
            pl.BlockSpec((tm, D_MODEL), lambda i: (i, 0)),
            _const_spec((1, D_MODEL)),
            _const_spec((D_MODEL, D_FF)),
            _const_spec((D_MODEL, D_FF)),
            _const_spec((D_FF, D_MODEL)),
            _const_spec((1, D_MODEL)),
        ],
        out_specs=pl.BlockSpec((tm, D_MODEL), lambda i: (i, 0)),
        out_shape=jax.ShapeDtypeStruct((n, D_MODEL), F32),
        scratch_shapes=[pltpu.VMEM((tm, D_MODEL), F32)],
        compiler_params=pltpu.CompilerParams(
            dimension_semantics=("parallel",), vmem_limit_bytes=V7X_VMEM_LIMIT),
        name="ffn_final" if final_norm else "ffn",
    )(x, g, wa, wb, wd, gf)


def _proj_kernel(x_ref, g_ref, w_ref, bias_ref, aq_ref, ak_ref, av_ref, iq_ref, ik_ref, misc_ref,
                 mq_ref, mk_ref, mv_ref, mo_ref, ga_ref, gb_ref):
    h = _rms(x_ref[...], g_ref[...]).astype(BF16)

    def cols(c0, width):
        return lax.dot_general(h, w_ref[c0:c0 + width, :], NT_DIMS, preferred_element_type=F32)

    aq_ref[...] = cols(C_AQ, 512).astype(BF16)
    kv = cols(C_KV, 128)
    ak_ref[...] = kv[:, :DH_A]
    av_ref[...] = kv[:, DH_A:]
    iq_ref[...] = cols(C_IQ, 256).astype(BF16)
    misc = cols(C_MISC, 128) + bias_ref[...]
    misc_ref[...] = misc
    ik_ref[...] = misc[:, :D_IDX]
    mq_ref[...] = cols(C_MQ, 256).astype(BF16)
    mk_ref[...] = cols(C_MK, 256).astype(BF16)
    mv_ref[...] = cols(C_MV, 512).astype(BF16)
    mo_ref[...] = cols(C_MO, 512).astype(BF16)
    for c in range(4):
        ga_ref[:, c * 256:(c + 1) * 256] = cols(C_GA + c * 256, 256).astype(BF16)
        gb_ref[:, c * 256:(c + 1) * 256] = cols(C_GB + c * 256, 256).astype(BF16)


def _proj(x, g, w, bias, *, tm):
    n = x.shape[0]
    widths = [(512, BF16), (DH_A, F32), (DH_A, F32), (256, BF16), (D_IDX, F32), (128, F32),
              (256, BF16), (256, BF16), (512, BF16), (512, BF16), (D_MODEL, BF16), (D_MODEL, BF16)]
    return pl.pallas_call(
        _proj_kernel,
        grid=(n // tm,),
        in_specs=[
            pl.BlockSpec((tm, D_MODEL), lambda i: (i, 0)),
            _const_spec((1, D_MODEL)),
            _const_spec((D_INR, D_MODEL)),
            _const_spec((1, 128)),
        ],
        out_specs=[pl.BlockSpec((tm, w), lambda i: (i, 0)) for w, _ in widths],
        out_shape=[jax.ShapeDtypeStruct((n, w), dt) for w, dt in widths],
        compiler_params=pltpu.CompilerParams(
            dimension_semantics=("parallel",), vmem_limit_bytes=V7X_VMEM_LIMIT),
        name="proj",
    )(x, g, w, bias)


KB = 128
N_ACC = 4
QX_ALIBI = DH_A


def _fold16(x, op):
    parts = [x[r:r + 16, :] for r in range(0, x.shape[0], 16)]
    while len(parts) > 1:
        parts = [op(parts[i], parts[i + 1]) for i in range(0, len(parts), 2)]
    return parts[0]


def _reduce_blocks(n_keys, block, op):
    accs = [None] * N_ACC
    for i, r in enumerate(range(0, n_keys, KB)):
        m = block(r)
        j = i % N_ACC
        accs[j] = m if accs[j] is None else op(accs[j], m)
    accs = [a for a in accs if a is not None]
    tot = accs[0]
    for a in accs[1:]:
        tot = op(tot, a)
    return tot


def _flag16(mask):
    return jnp.where(mask, jnp.int16(1), jnp.int16(0))


def _sum16(n_keys, flags):
    tot = _reduce_blocks(n_keys, lambda r: _fold16(flags(r), jnp.add), jnp.add)
    return jnp.sum(tot.astype(jnp.int32).astype(F32), axis=0, keepdims=True)


def _kth_largest16(ref, need, n_keys, total):
    c = _sum16(n_keys, lambda r: _flag16(ref[r:r + KB, :] >= jnp.int16(0)))
    ok = c >= need
    t = jnp.where(ok, 0, INT16_MIN).astype(jnp.int32)
    cnt = jnp.where(ok, c, total)
    for b in range(14, -1, -1):
        cand = t + (1 << b)
        cand16 = cand.astype(jnp.int16)
        c = _sum16(n_keys, lambda r, cand16=cand16: _flag16(ref[r:r + KB, :] >= cand16))
        ok = c >= need
        t = jnp.where(ok, cand, t)
        cnt = jnp.where(ok, c, cnt)
    return t, cnt


def _store_keys(score, hi_ref, lo_ref, r, admissible, lanes=slice(None)):
    bits = pltpu.bitcast(score, jnp.int32)
    key = bits ^ ((bits >> 31) & 0x7FFFFFFF)
    key = jnp.where(key == -1, 0, key)
    if admissible is not None:
        key = jnp.where(admissible, key, KEY_NEG_INF)
    hi_ref[r:r + KB, lanes] = (key >> 16).astype(jnp.int16)
    lo_ref[r:r + KB, lanes] = (key ^ 0x8000).astype(jnp.int16)


def _select_mask(hi_ref, lo_ref, mb_ref, p_ref, n_keys, nl, top_k, valid16):
    kf = float(top_k)
    t_hi, c_ge_hi = _kth_largest16(hi_ref, kf, n_keys, float(n_keys))
    t_hi16 = t_hi.astype(jnp.int16)
    need_lo = kf - _sum16(n_keys, lambda r: _flag16(hi_ref[r:r + KB, :] > t_hi16))
    for r in range(0, n_keys, KB):
        lo_ref[r:r + KB, :] = jnp.where(hi_ref[r:r + KB, :] == t_hi16, lo_ref[r:r + KB, :], jnp.int16(INT16_MIN))
    n_cand = c_ge_hi - (kf - need_lo)
    t_lo, c_ge_lo = _kth_largest16(lo_ref, need_lo, n_keys, n_cand)
    t_lo16 = t_lo.astype(jnp.int16)

    def key_idx16(r):
        return (r + lax.broadcasted_iota(jnp.int32, (KB, nl), 0)).astype(jnp.int16)

    p_ref[...] = jnp.full((1, nl), 32767, jnp.int32)

    @pl.when(jnp.max(c_ge_lo - need_lo) > 0.0)
    def _():
        def eq_flags(r):
            at_lo = _flag16(lo_ref[r:r + KB, :] == t_lo16)
            return jnp.where(hi_ref[r:r + KB, :] == t_hi16, at_lo, jnp.int16(0))

        need_eq = need_lo - _sum16(n_keys, lambda r: _flag16(lo_ref[r:r + KB, :] > t_lo16))
        p = jnp.zeros((1, nl), jnp.int32)
        bit = 1 << ((n_keys - 1).bit_length() - 1)
        while bit:
            cand = p + bit
            cand16 = cand.astype(jnp.int16)
            c = _sum16(n_keys, lambda r, cand16=cand16: jnp.where(key_idx16(r) < cand16, eq_flags(r), jnp.int16(0)))
            p = jnp.where(c < need_eq, cand, p)
            bit >>= 1
        p_ref[...] = p

    p16 = p_ref[...].astype(jnp.int16)
    zero16 = jnp.asarray(0.0, BF16)
    ninf16 = jnp.asarray(float("-inf"), BF16)
    for r in range(0, n_keys, KB):
        hi = hi_ref[r:r + KB, :]
        lo = lo_ref[r:r + KB, :]
        at_thr = jnp.where(lo == t_lo16, jnp.where(key_idx16(r) <= p16, zero16, ninf16), ninf16)
        in_class = jnp.where(lo > t_lo16, zero16, at_thr)
        mb = jnp.where(hi > t_hi16, zero16, jnp.where(hi == t_hi16, in_class, ninf16))
        valid = valid16(r)
        if valid is not None:
            mb = jnp.where(valid, mb, ninf16)
        mb_ref[r:r + KB, :] = mb


def _store_logits(kx_ref, lg_ref, qx, fix, n_keys):
    for r in range(0, n_keys, KB):
        s = lax.dot_general(kx_ref[r:r + KB, :], qx, NT_DIMS, preferred_element_type=F32)
        f = fix(r)
        lg_ref[r:r + KB, :] = s if f is None else s + f


def _attend_max(lg_ref, mb_ref, n_keys, lanes=slice(None)):
    mx = _reduce_blocks(
        n_keys, lambda r: _fold16(lg_ref[r:r + KB, :].astype(BF16) + mb_ref[r:r + KB, lanes], jnp.maximum),
        jnp.maximum)
    return jnp.max(mx.astype(F32), axis=0, keepdims=True)


def _attend_exp(lg_ref, mb_ref, pt_ref, mx, n_keys, lanes=slice(None)):
    for r in range(0, n_keys, KB):
        pt_ref[r:r + KB, :] = jnp.exp2((lg_ref[r:r + KB, :] - mx).astype(BF16) + mb_ref[r:r + KB, lanes])


def _attend_pv(vt_ref, pt_ref):
    return jnp.dot(vt_ref[...], pt_ref[...], preferred_element_type=F32)


def _attend_slabs(n_slabs, maxf, expf, pvf):
    mx = {}
    for step in range(n_slabs + 2):
        if step < n_slabs:
            mx[step] = maxf(step)
        if 0 <= step - 1 < n_slabs:
            expf(step - 1, mx.pop(step - 1))
        if 0 <= step - 2 < n_slabs:
            pvf(step - 2)


def _alibi_key_columns(n_keys):
    s_pos = lax.broadcasted_iota(jnp.int32, (n_keys, DH_A), 0)
    ln = lax.broadcasted_iota(jnp.int32, (n_keys, DH_A), 1)
    s_part = jnp.where((ln & 1) == 0, s_pos >> 6, s_pos & (CHUNK - 1))
    return jnp.where(ln < 4, s_part, 0).astype(F32).astype(BF16)


def _ones_column(n_keys):
    ln = lax.broadcasted_iota(jnp.int32, (n_keys, DH_A), 1)
    return jnp.where(ln == 0, 1.0, 0.0)


def _pick_rows(x, lane0):
    lane = lax.broadcasted_iota(jnp.int32, (8, 128), 1)
    sub = lax.broadcasted_iota(jnp.int32, (8, 128), 0)
    pick = jnp.where(lane == sub + lane0, 1.0, 0.0)
    return lax.dot_general(pick, x, NT_DIMS, preferred_element_type=F32, precision=lax.Precision.HIGHEST)


def _dsa_prompt_kernel(aq_ref, iq_ref, misc_ref, acoef_ref, k_ref, v_ref, ik_ref, *rest, tq, n_keys, top_k, has_prev):
    (o_ref, kx_ref, vx_ref, vt_ref, qx_ref, iqh_ref, hi_ref, lo_ref, mb_ref, fix_ref, lg_ref, pt_ref, ot_ref,
     p_ref) = rest[1:] if has_prev else rest
    q0 = n_keys - tq

    kx_ref[:, :DH_A] = k_ref[0].astype(BF16)
    kx_ref[:, DH_A:] = _alibi_key_columns(n_keys)
    vx_ref[:, :DH_A] = v_ref[0]
    vx_ref[:, DH_A:] = _ones_column(n_keys)
    vt_ref[...] = vx_ref[...].T.astype(BF16)
    for h in range(N_HEADS_A):
        qx_ref[h, :, :DH_A] = aq_ref[:, h * DH_A:(h + 1) * DH_A]
        qx_ref[h, :, DH_A:] = jnp.broadcast_to(acoef_ref[h:h + 1, DH_A:], (tq, DH_A)).astype(BF16)
    for h in range(N_IDX):
        iqh_ref[h] = iq_ref[:, h * D_IDX:(h + 1) * D_IDX]

    iw_rows = _pick_rows(misc_ref[...], MISC_IW) * IDX_SCALE
    t_pos = q0 + lax.broadcasted_iota(jnp.int32, (KB, tq), 1)

    def key_pos(r):
        return r + lax.broadcasted_iota(jnp.int32, (KB, tq), 0)

    for r in range(0, n_keys, KB):
        ikb = ik_ref[0, r:r + KB, :].astype(BF16)
        score = None
        for h in range(N_IDX):
            rel = lax.dot_general(ikb, iqh_ref[h], NT_DIMS, preferred_element_type=F32)
            rel = jnp.maximum(rel, 0.0) * iw_rows[h:h + 1, :]
            score = rel if score is None else score + rel
        adm = None
        if r >= q0:
            adm = (key_pos(r) >> 6) <= (t_pos >> 6)
            fix_ref[r - q0:r - q0 + KB, :] = jnp.maximum(key_pos(r) - t_pos, 0).astype(F32)
        _store_keys(score, hi_ref, lo_ref, r, adm)

    for h in range(N_HEADS_A):
        fix_scale = -2.0 * LOG2E * 2.0 ** (-8.0 * (h + 1) / N_HEADS_A)
        _store_logits(kx_ref, lg_ref.at[h], qx_ref[h],
                      lambda r, c=fix_scale: c * fix_ref[r - q0:r - q0 + KB, :] if r >= q0 else None, n_keys)

    def valid16(r):
        if r < q0:
            return None
        return (key_pos(r) >> 6).astype(jnp.int16) <= (t_pos >> 6).astype(jnp.int16)

    _select_mask(hi_ref, lo_ref, mb_ref, p_ref, n_keys, tq, top_k, valid16)

    def pv(h):
        ot = _attend_pv(vt_ref, pt_ref.at[h])
        ot_ref[h * DH_A:(h + 1) * DH_A, :] = ot[:DH_A, :] / ot[DH_A:DH_A + 1, :]

    _attend_slabs(
        N_HEADS_A,
        lambda h: _attend_max(lg_ref.at[h], mb_ref, n_keys),
        lambda h, mx: _attend_exp(lg_ref.at[h], mb_ref, pt_ref.at[h], mx, n_keys),
        pv)
    o_ref[...] = ot_ref[...].T.astype(BF16)


def _dsa_prompt(aq, iq, misc, acoef, k, v, ik, prev, *, batch, q_len, tq, q0, top_k):
    n_keys = q0 + tq
    per_b = q_len // tq
    blk0 = q0 // tq

    def qmap(b):
        return (b * per_b + blk0, 0)

    kvspec = pl.BlockSpec((1, n_keys, DH_A), lambda b: (b, 0, 0))
    operands = (aq, iq, misc, acoef, k, v, ik) + (() if prev is None else (prev,))
    return pl.pallas_call(
        functools.partial(_dsa_prompt_kernel, tq=tq, n_keys=n_keys, top_k=top_k, has_prev=prev is not None),
        grid=(batch,),
        in_specs=[
            pl.BlockSpec((tq, N_HEADS_A * DH_A), qmap),
            pl.BlockSpec((tq, N_IDX * D_IDX), qmap),
            pl.BlockSpec((tq, 128), qmap),
            pl.BlockSpec((N_HEADS_A, 128), lambda b: (0, 0)),
            kvspec, kvspec, kvspec,
        ] + ([] if prev is None else [pl.BlockSpec(memory_space=pl.ANY)]),
        out_specs=pl.BlockSpec((tq, N_HEADS_A * DH_A), qmap),
        out_shape=jax.ShapeDtypeStruct((batch * q_len, N_HEADS_A * DH_A), BF16),
        input_output_aliases={} if prev is None else {7: 0},
        scratch_shapes=[
            pltpu.VMEM((n_keys, 128), BF16),
            pltpu.VMEM((n_keys, 128), F32),
            pltpu.VMEM((128, n_keys), BF16),
            pltpu.VMEM((N_HEADS_A, tq, 128), BF16),
            pltpu.VMEM((N_IDX, tq, D_IDX), BF16),
            pltpu.VMEM((n_keys, tq), jnp.int16),
            pltpu.VMEM((n_keys, tq), jnp.int16),
            pltpu.VMEM((n_keys, tq), BF16),
            pltpu.VMEM((tq, tq), F32),
            pltpu.VMEM((N_HEADS_A, n_keys, tq), F32),
            pltpu.VMEM((N_HEADS_A, n_keys, tq), BF16),
            pltpu.VMEM((N_HEADS_A * DH_A, tq), F32),
            pltpu.VMEM((1, tq), jnp.int32),
        ],
        compiler_params=pltpu.CompilerParams(
            dimension_semantics=("parallel",), vmem_limit_bytes=V7X_VMEM_LIMIT),
        name="dsa_prompt",
    )(*operands)


def _dsa_sample_kernel(aq_ref, iq_ref, misc_ref, acoef_ref, ck_ref, cv_ref, cik_ref, nk_ref, nv_ref, nik_ref, o_ref,
                       kx_ref, ikx_ref, vx_ref, vt_ref, qx_ref, iqx_ref, hi_ref, lo_ref, mb_ref, lg_ref, pt_ref, p_ref,
                       *, group, t, past, n_keys, top_k):
    total = past + t
    nl = N_HEADS_A * t
    last = n_keys - KB
    assert past % KB == 0 and total <= n_keys and past >= last and (past + t - 1) // CHUNK == past // CHUNK
    lane = lax.broadcasted_iota(jnp.int32, (KB, nl), 1)
    row = last + lax.broadcasted_iota(jnp.int32, (KB, nl), 0)
    real = row < total
    slope = jnp.exp2(-8.0 * ((lane // t) + 1).astype(F32) / N_HEADS_A)
    fix_last = (-2.0 * LOG2E) * slope * jnp.maximum(row - (past + lane % t), 0).astype(F32)
    sub = lax.broadcasted_iota(jnp.int32, (t, nl), 0)
    same_q = (lax.broadcasted_iota(jnp.int32, (t, nl), 1) % t) == sub

    for g in range(group):
        rows = slice(g * t, (g + 1) * t)
        lanes = slice(g * nl, (g + 1) * nl)
        kx, ikx, vx = kx_ref.at[g], ikx_ref.at[g], vx_ref.at[g]
        for dst, cache, new in ((kx, ck_ref, nk_ref), (ikx, cik_ref, nik_ref), (vx, cv_ref, nv_ref)):
            dst[:past, :DH_A] = cache[g].astype(dst.dtype)
            dst[past:total, :DH_A] = new[g].astype(dst.dtype)
            dst[total:, :DH_A] = jnp.zeros((n_keys - total, DH_A), dst.dtype)
        kx[:, DH_A:] = _alibi_key_columns(n_keys)
        vx[:, DH_A:] = _ones_column(n_keys)
        vt_ref[g] = vx[...].T.astype(BF16)
        for h in range(N_HEADS_A):
            qx_ref[g, h * t:(h + 1) * t, :DH_A] = aq_ref[rows, h * DH_A:(h + 1) * DH_A]
            qx_ref[g, h * t:(h + 1) * t, DH_A:] = jnp.broadcast_to(acoef_ref[h:h + 1, DH_A:], (t, DH_A)).astype(BF16)
        for h in range(N_IDX):
            iqx_ref[g, h * t:(h + 1) * t, :] = iq_ref[rows, h * D_IDX:(h + 1) * D_IDX]
        iqx_ref[g, N_IDX * t:, :] = jnp.zeros((nl - N_IDX * t, D_IDX), BF16)

        misc = misc_ref[rows, :]
        wmat = jnp.concatenate(
            [jnp.where(same_q, misc[:, MISC_IW + h:MISC_IW + h + 1] * IDX_SCALE, 0.0) for h in range(N_IDX)]
            + [jnp.zeros((nl - N_IDX * t, nl), F32)], axis=0)
        for r in range(0, n_keys, KB):
            rel = lax.dot_general(ikx[r:r + KB, :], iqx_ref[g], NT_DIMS, preferred_element_type=F32)
            score = jnp.dot(jnp.maximum(rel, 0.0), wmat, preferred_element_type=F32,
                            precision=lax.Precision.HIGHEST)
            _store_keys(score, hi_ref, lo_ref, r, real if r == last else None, lanes)
        _store_logits(kx, lg_ref.at[g], qx_ref[g], lambda r: fix_last if r == last else None, n_keys)

    real16 = (last + lax.broadcasted_iota(jnp.int32, (KB, group * nl), 0)).astype(jnp.int16) < jnp.int16(total)
    _select_mask(hi_ref, lo_ref, mb_ref, p_ref, n_keys, group * nl, top_k, lambda r: real16 if r == last else None)

    def lanes_of(g):
        return slice(g * nl, (g + 1) * nl)

    def pv(g):
        res = _attend_pv(vt_ref.at[g], pt_ref.at[g]).T
        res = res[:, :DH_A] / res[:, DH_A:DH_A + 1]
        for h in range(N_HEADS_A):
            o_ref[g * t:(g + 1) * t, h * DH_A:(h + 1) * DH_A] = res[h * t:(h + 1) * t, :].astype(BF16)

    _attend_slabs(
        group,
        lambda g: _attend_max(lg_ref.at[g], mb_ref, n_keys, lanes_of(g)),
        lambda g, mx: _attend_exp(lg_ref.at[g], mb_ref, pt_ref.at[g], mx, n_keys, lanes_of(g)),
        pv)


def _dsa_sample(aq, iq, misc, acoef, ck, cv, cik, nk, nv, nik, *, batch, t, group, top_k):
    past = ck.shape[1]
    n_keys = -(-(past + t) // KB) * KB
    nl = N_HEADS_A * t
    assert nl == 128 and batch % group == 0

    def tok(w):
        return pl.BlockSpec((group * t, w), lambda i: (i, 0))

    cspec = pl.BlockSpec((group, past, DH_A), lambda i: (i, 0, 0))
    nspec = pl.BlockSpec((group, t, DH_A), lambda i: (i, 0, 0))
    return pl.pallas_call(
        functools.partial(_dsa_sample_kernel, group=group, t=t, past=past, n_keys=n_keys, top_k=top_k),
        grid=(batch // group,),
        in_specs=[tok(N_HEADS_A * DH_A), tok(N_IDX * D_IDX), tok(128),
                  pl.BlockSpec((N_HEADS_A, 128), lambda i: (0, 0)),
                  cspec, cspec, cspec, nspec, nspec, nspec],
        out_specs=tok(N_HEADS_A * DH_A),
        out_shape=jax.ShapeDtypeStruct((batch * t, N_HEADS_A * DH_A), BF16),
        scratch_shapes=[
            pltpu.VMEM((group, n_keys, 128), BF16),
            pltpu.VMEM((group, n_keys, D_IDX), BF16),
            pltpu.VMEM((group, n_keys, 128), F32),
            pltpu.VMEM((group, 128, n_keys), BF16),
            pltpu.VMEM((group, nl, 128), BF16),
            pltpu.VMEM((group, nl, D_IDX), BF16),
            pltpu.VMEM((n_keys, group * nl), jnp.int16),
            pltpu.VMEM((n_keys, group * nl), jnp.int16),
            pltpu.VMEM((n_keys, group * nl), BF16),
            pltpu.VMEM((group, n_keys, nl), F32),
            pltpu.VMEM((group, n_keys, nl), BF16),
            pltpu.VMEM((1, group * nl), jnp.int32),
        ],
        compiler_params=pltpu.CompilerParams(
            dimension_semantics=("parallel",), vmem_limit_bytes=V7X_VMEM_LIMIT),
        name="dsa_sample",
    )(aq, iq, misc, acoef, ck, cv, cik, nk, nv, nik)


def _alibi_coef():
    slopes = jnp.exp2(-8.0 * jnp.arange(1, N_HEADS_A + 1, dtype=F32) / N_HEADS_A) * LOG2E
    a_hi = slopes.astype(BF16)
    a_lo = (slopes - a_hi.astype(F32)).astype(BF16)
    cols = jnp.stack([a_hi.astype(F32) * CHUNK, a_hi.astype(F32), a_lo.astype(F32) * CHUNK, a_lo.astype(F32)], axis=1)
    return jnp.zeros((N_HEADS_A, 128), F32).at[:, QX_ALIBI:QX_ALIBI + 4].set(cols)


def _mlstm_kernel(mq_ref, mk_ref, mv_ref, misc_ref, mo_ref, gn_ref, c0_ref, n0_ref, m0_ref,
                  o_ref, c_out, n_out, m_out, c_ref, n_ref, m_ref, *, lc):
    ci = pl.program_id(1)

    @pl.when(ci == 0)
    def _():
        c_ref[...] = c0_ref[0]
        n_ref[...] = n0_ref[0]
        m_ref[...] = m0_ref[0]

    heads = range(NH_M)
    misc = misc_ref[...]
    g_rows = _pick_rows(misc, MISC_IG)
    g_cols = misc[:, MISC_IG:MISC_IG + 2 * NH_M]

    def log_sigmoid(x):
        return jnp.minimum(x, 0.0) - jnp.log1p(jnp.exp(-jnp.abs(x)))

    lf_rows = log_sigmoid(g_rows[NH_M:, :])
    lf_cols = log_sigmoid(g_cols[:, NH_M:])
    si = lax.broadcasted_iota(jnp.int32, (lc, lc), 0)
    ti = lax.broadcasted_iota(jnp.int32, (lc, lc), 1)
    causal = si <= ti
    upper = jnp.where(causal, 1.0, 0.0)
    b_rows = jnp.dot(lf_rows, upper, preferred_element_type=F32, precision=lax.Precision.HIGHEST)
    b_cols = lax.dot_general(upper, lf_cols, TN_DIMS, preferred_element_type=F32,
                             precision=lax.Precision.HIGHEST)
    c_rows = g_rows[:NH_M, :] - b_rows
    c_cols = g_cols[:, :NH_M] - b_cols
    ninf = float("-inf")

    q = [mq_ref[:, h * DK_M:(h + 1) * DK_M] for h in heads]
    k = [mk_ref[:, h * DK_M:(h + 1) * DK_M] for h in heads]
    m_prev = [m_ref[:, h:h + 1] for h in heads]
    c_prev = [c_ref[h] for h in heads]
    n_prev = [n_ref[h:h + 1, :] for h in heads]

    d = [jnp.where(causal, b_rows[h:h + 1, :] + c_cols[:, h:h + 1], ninf) for h in heads]
    g = [b_rows[h:h + 1, :] + m_prev[h] for h in heads]
    m_t = [jnp.maximum(g[h], jnp.max(d[h], axis=0, keepdims=True)) for h in heads]
    dw = [jnp.exp(d[h] - m_t[h]) for h in heads]
    gw = [jnp.exp(g[h] - m_t[h]) for h in heads]
    st = [lax.dot_general(k[h], q[h], NT_DIMS, preferred_element_type=F32) * dw[h] for h in heads]
    qc = [lax.dot_general(c_prev[h].astype(BF16), q[h], NT_DIMS, preferred_element_type=F32) for h in heads]
    qf = [q[h].astype(F32) for h in heads]
    nqc = [lax.dot_general(jnp.broadcast_to(n_prev[h], (8, DK_M)), qf[h], NT_DIMS, preferred_element_type=F32,
                           precision=lax.Precision.HIGHEST)[0:1, :] for h in heads]
    vt = [mv_ref[:, h * DV_M:(h + 1) * DV_M].astype(F32).T for h in heads]
    num = [jnp.dot(vt[h].astype(BF16), st[h].astype(BF16), preferred_element_type=F32) + gw[h] * qc[h]
           for h in heads]
    nq = [jnp.sum(st[h], axis=0, keepdims=True) + gw[h] * nqc[h] for h in heads]
    for h in heads:
        hh = num[h] / jnp.maximum(jnp.abs(nq[h]), jnp.exp(-m_t[h]))
        hh = hh * lax.rsqrt(jnp.mean(hh * hh, axis=0, keepdims=True) + EPS)
        hsl = slice(h * DV_M, (h + 1) * DV_M)
        gate = gn_ref[:, hsl] * jax.nn.sigmoid(mo_ref[:, hsl].astype(F32))
        o_ref[:, hsl] = (hh.T * gate).astype(BF16)
    for h in heads:
        b_last = b_rows[h:h + 1, lc - 1:lc]
        m_last = m_t[h][:, lc - 1:lc]
        decay = gw[h][:, lc - 1:lc]
        w_row = jnp.exp(c_rows[h:h + 1, :] + (b_last - m_last))
        vw = (vt[h] * w_row).astype(BF16)
        c_ref[h] = decay * c_prev[h] + jnp.dot(vw, k[h], preferred_element_type=F32)
        wk = jnp.dot(jnp.broadcast_to(w_row, (8, lc)), k[h].astype(F32), preferred_element_type=F32,
                     precision=lax.Precision.HIGHEST)[0:1, :]
        n_ref[h:h + 1, :] = decay * n_prev[h] + wk
        m_ref[:, h:h + 1] = m_last

    @pl.when(ci == pl.num_programs(1) - 1)
    def _():
        c_out[0] = c_ref[...]
        n_out[0] = n_ref[...]
        m_out[0] = m_ref[...]


def _mlstm(mq, mk, mv, misc, mo, gn, c0, n0, m0, *, batch, seq, lc):
    nc = seq // lc

    def tmap(b, c):
        return (b * nc + c, 0)

    def smap4(b, c):
        return (b, 0, 0, 0)

    def smap3(b, c):
        return (b, 0, 0)

    return pl.pallas_call(
        functools.partial(_mlstm_kernel, lc=lc),
        grid=(batch, nc),
        in_specs=[
            pl.BlockSpec((lc, NH_M * DK_M), tmap),
            pl.BlockSpec((lc, NH_M * DK_M), tmap),
            pl.BlockSpec((lc, NH_M * DV_M), tmap),
            pl.BlockSpec((lc, 128), tmap),
            pl.BlockSpec((lc, NH_M * DV_M), tmap),
            pl.BlockSpec((1, NH_M * DV_M), lambda b, c: (0, 0)),
            pl.BlockSpec((1, NH_M, DV_M, DK_M), smap4),
            pl.BlockSpec((1, NH_M, DK_M), smap3),
            pl.BlockSpec((1, 1, NH_M), smap3),
        ],
        out_specs=[
            pl.BlockSpec((lc, NH_M * DV_M), tmap),
            pl.BlockSpec((1, NH_M, DV_M, DK_M), smap4),
            pl.BlockSpec((1, NH_M, DK_M), smap3),
            pl.BlockSpec((1, 1, NH_M), smap3),
        ],
        out_shape=[
            jax.ShapeDtypeStruct((batch * seq, NH_M * DV_M), BF16),
            jax.ShapeDtypeStruct((batch, NH_M, DV_M, DK_M), F32),
            jax.ShapeDtypeStruct((batch, NH_M, DK_M), F32),
            jax.ShapeDtypeStruct((batch, 1, NH_M), F32),
        ],
        scratch_shapes=[
            pltpu.VMEM((NH_M, DV_M, DK_M), F32),
            pltpu.VMEM((NH_M, DK_M), F32),
            pltpu.VMEM((1, NH_M), F32),
        ],
        compiler_params=pltpu.CompilerParams(
            dimension_semantics=("parallel", "arbitrary"), vmem_limit_bytes=V7X_VMEM_LIMIT),
        name="mlstm",
    )(mq, mk, mv, misc, mo, gn, c0, n0, m0)


def _mix_kernel(x_ref, attn_ref, hm_ref, ga_ref, gb_ref, wba_ref, wbb_ref, wo_ref, o_ref):
    a = jnp.dot(attn_ref[...], wba_ref[...], preferred_element_type=F32)
    b = jnp.dot(hm_ref[...], wbb_ref[...], preferred_element_type=F32)
    y = jax.nn.sigmoid(ga_ref[...].astype(F32)) * a + jax.nn.sigmoid(gb_ref[...].astype(F32)) * b
    o_ref[...] = x_ref[...] + jnp.dot(y.astype(BF16), wo_ref[...], preferred_element_type=F32)


def _mix(x, attn, hm, ga, gb, wba, wbb, wo, *, tm):
    n = x.shape[0]

    def tok(w):
        return pl.BlockSpec((tm, w), lambda i: (i, 0))

    return pl.pallas_call(
        _mix_kernel,
        grid=(n // tm,),
        in_specs=[tok(D_MODEL), tok(512), tok(512), tok(D_MODEL), tok(D_MODEL),
                  _const_spec((512, D_MODEL)), _const_spec((512, D_MODEL)), _const_spec((D_MODEL, D_MODEL))],
        out_specs=tok(D_MODEL),
        out_shape=jax.ShapeDtypeStruct((n, D_MODEL), F32),
        compiler_params=pltpu.CompilerParams(
            dimension_semantics=("parallel",), vmem_limit_bytes=V7X_VMEM_LIMIT),
        name="mix",
    )(x, attn, hm, ga, gb, wba, wbb, wo)


DSA_TQ = 256
DSA_SAMPLE_GROUP = 4
MLSTM_CHUNK = 256


def _regroup_w_in(w_in, b_gates):
    offs = [0]
    for s in SPLITS:
        offs.append(offs[-1] + s)
    scale = [1.0] * len(SPLITS)
    scale[0] = DH_A ** -0.5 * LOG2E
    scale[6] = DK_M ** -0.5
    row_scale = jnp.concatenate([jnp.full((n, 1), c, F32) for n, c in zip(SPLITS, scale)], axis=0)
    wt = (jnp.swapaxes(w_in, 0, 1) * row_scale).astype(BF16)
    aq, ak, av, iq, ik, iw, mq, mk, mv, mif, mo, gates = [wt[offs[i]:offs[i + 1]] for i in range(len(SPLITS))]
    pad = jnp.zeros((128 - D_IDX - N_IDX - 2 * NH_M, D_MODEL), BF16)
    w = jnp.concatenate([aq, ak, av, iq, ik, iw, mif, pad, mq, mk, mv, mo, gates], axis=0)
    bias = jnp.zeros((1, 128), F32).at[0, MISC_IG:MISC_IG + 2 * NH_M].set(b_gates.astype(F32))
    return w, bias


def _group(x, layer, *, batch, seq, tm, prompt, cache=None, state=None):
    n = batch * seq
    x = x.reshape(n, D_MODEL)
    x1 = _ffn(x, layer["g1"], *layer["ffn1"], layer["gf"], final_norm=False, tm=tm)
    aq, ak, av, iq, ik, misc, mq, mk, mv, mo, ga, gb = _proj(x1, layer["gmix"], layer["w_in"], layer["bias"], tm=tm)
    ak3, av3, ik3 = (a.reshape(batch, seq, DH_A) for a in (ak, av, ik))
    acoef = _alibi_coef()
    if prompt:
        top_k = min(TOPK_MAX, seq // 4)
        attn = None
        for q0 in range(0, seq, DSA_TQ):
            attn = _dsa_prompt(aq, iq, misc, acoef, ak3, av3, ik3, attn, batch=batch, q_len=seq, tq=DSA_TQ, q0=q0,
                               top_k=top_k)
        c0 = jnp.zeros((batch, NH_M, DV_M, DK_M), F32)
        n0 = jnp.zeros((batch, NH_M, DK_M), F32)
        m0 = jnp.zeros((batch, 1, NH_M), F32)
        lc = MLSTM_CHUNK
    else:
        ck, cv, cik = cache
        attn = _dsa_sample(aq, iq, misc, acoef, ck, cv, cik, ak3, av3, ik3, batch=batch, t=seq, group=DSA_SAMPLE_GROUP,
                           top_k=min(TOPK_MAX, (ck.shape[1] + seq) // 4))
        c0, n0, m0 = state
        c0 = c0.astype(F32)
        n0 = n0.astype(F32)
        m0 = m0.astype(F32).reshape(batch, 1, NH_M)
        lc = seq
    hm, c_new, n_new, m_new = _mlstm(mq, mk, mv, misc, mo, layer["gn"], c0, n0, m0, batch=batch, seq=seq, lc=lc)
    x2 = _mix(x1, attn, hm, ga, gb, layer["wba"], layer["wbb"], layer["wo"], tm=tm)
    y = _ffn(x2, layer["g2"], *layer["ffn2"], layer["gf"], final_norm=True, tm=tm)
    return (y.reshape(batch, seq, D_MODEL), ak3, av3, ik3, c_new, n_new, m_new.reshape(batch, NH_M))


def kernel(x_prompt, x_sample, cache_attn_k, cache_attn_v, cache_idx_k, state_mlstm_C, state_mlstm_n,
           state_mlstm_m, norm_ffn1, w_ffn1_up, w_ffn1_down, norm_mix, w_in, b_mlstm_gates, norm_mlstm_heads,
           w_branch_attn, w_branch_mlstm, w_out, norm_ffn2, w_ffn2_up, w_ffn2_down, norm_final):
    assert w_in.shape[0] == 1, "single-layer trunk"
    bp, tp, _ = x_prompt.shape
    bs, ts, _ = x_sample.shape

    def ffn_w(w_up, w_down):
        return (w_up[0, :, :D_FF].astype(BF16), w_up[0, :, D_FF:].astype(BF16), w_down[0].astype(BF16))

    w_r, bias = _regroup_w_in(w_in[0], b_mlstm_gates[0])
    layer = dict(
        g1=norm_ffn1[0].reshape(1, -1), ffn1=ffn_w(w_ffn1_up, w_ffn1_down),
        g2=norm_ffn2[0].reshape(1, -1), ffn2=ffn_w(w_ffn2_up, w_ffn2_down),
        gmix=norm_mix[0].reshape(1, -1), w_in=w_r, bias=bias,
        gn=norm_mlstm_heads[0].reshape(1, -1).astype(F32),
        wba=w_branch_attn[0].astype(BF16), wbb=w_branch_mlstm[0].astype(BF16), wo=w_out[0].astype(BF16),
        gf=norm_final.reshape(1, -1),
    )
    yp, kp, vp, ikp, cp, np_, mp = _group(x_prompt, layer, batch=bp, seq=tp, tm=512, prompt=True)
    ys, ks, vs, iks, cs, ns, ms = _group(
        x_sample, layer, batch=bs, seq=ts, tm=bs * ts, prompt=False,
        cache=(cache_attn_k[0], cache_attn_v[0], cache_idx_k[0]),
        state=(state_mlstm_C[0], state_mlstm_n[0], state_mlstm_m[0]))
    dk, dv, di = cache_attn_k.dtype, cache_attn_v.dtype, cache_idx_k.dtype
    dc, dn, dm = state_mlstm_C.dtype, state_mlstm_n.dtype, state_mlstm_m.dtype
    return (yp, ys,
            kp[None].astype(dk), vp[None].astype(dv), ikp[None].astype(di),
            cp[None].astype(dc), np_[None].astype(dn), mp[None].astype(dm),
            ks[None].astype(dk), vs[None].astype(dv), iks[None].astype(di),
            cs[None].astype(dc), ns[None].astype(dn), ms[None].astype(dm))
```
